```python
import math
import jax, jax.numpy as jnp
from jax import lax
import numpy as np

D_MODEL = 2048
BATCH = 8
SEQ = 2048
DEPTH = 2
DEC_BATCH = 16
DEC_SEQ = 16
PAST_LEN = 1024

CHUNK = 64
HEAD_DIM = 128
FOX_HEADS = 8
FOX_WIDTH = FOX_HEADS * HEAD_DIM
GDN_HEADS = 8
GDN_DK = 128
GDN_DV = 128
GDN_QK_WIDTH = GDN_HEADS * GDN_DK
GDN_V_WIDTH = GDN_HEADS * GDN_DV
GDN_CONV_CH = 2 * GDN_QK_WIDTH + GDN_V_WIDTH
CONV_W = 4
MIX_WIDTH = FOX_WIDTH + GDN_V_WIDTH
IN_SPLITS = (FOX_WIDTH, FOX_WIDTH, FOX_WIDTH, FOX_HEADS,
             GDN_QK_WIDTH, GDN_QK_WIDTH, GDN_V_WIDTH, GDN_HEADS, GDN_HEADS, GDN_V_WIDTH)
IN_COLS = sum(IN_SPLITS)
QBLK = 128
D_FF = ((8 * D_MODEL + 3 * 256 - 1) // (3 * 256)) * 256
DEEPNORM_ALPHA = (2 * DEPTH) ** 0.25
DEEPNORM_BETA = (8 * DEPTH) ** -0.25
FGATE_BIAS_MEAN = 2.0
DT_MIN = 0.001
DT_MAX = 0.1
LN_EPS = 1e-5
RMS_EPS = 1e-6
L2_EPS = 1e-6

kernel_name = 'hybrid_fox_gdn_streaming_encoder_step'


def layer_norm(x, g, b):
    xf = x.astype(jnp.float32)
    mu = jnp.mean(xf, axis=-1, keepdims=True)
    var = jnp.mean(jnp.square(xf - mu), axis=-1, keepdims=True)
    y = (xf - mu) * lax.rsqrt(var + LN_EPS) * g.astype(jnp.float32) + b.astype(jnp.float32)
    return y.astype(x.dtype)


def rms_norm(x, g):
    xf = x.astype(jnp.float32)
    y = xf * lax.rsqrt(jnp.mean(jnp.square(xf), axis=-1, keepdims=True) + RMS_EPS) * g.astype(jnp.float32)
    return y.astype(x.dtype)


def l2_normalize(x):
    xf = x.astype(jnp.float32)
    return xf * lax.rsqrt(jnp.sum(jnp.square(xf), axis=-1, keepdims=True) + L2_EPS)


def fox_block(q, c_q, pos_q, k, v, c_k, pos_k):
    s = jnp.einsum('bqhd,bkhd->bhqk', q, k).astype(jnp.float32) * (HEAD_DIM ** -0.5)
    s = s + jnp.transpose(c_q, (0, 2, 1))[..., :, None] - jnp.transpose(c_k, (0, 2, 1))[..., None, :]
    mask = pos_q[:, None] >= pos_k[None, :]
    s = jnp.where(mask, s, -jnp.inf)
    p = jax.nn.softmax(s, axis=-1).astype(v.dtype)
    return jnp.einsum('bhqk,bkhd->bqhd', p, v)


def fox_prompt(q, k, v, logf):
    bsz, t_ = q.shape[:2]
    c = jnp.cumsum(logf, axis=1)
    pos = jnp.arange(t_)
    nb = t_ // QBLK
    qb = q.reshape(bsz, nb, QBLK, FOX_HEADS, HEAD_DIM).transpose(1, 0, 2, 3, 4)
    cb = c.reshape(bsz, nb, QBLK, FOX_HEADS).transpose(1, 0, 2, 3)
    pb = pos.reshape(nb, QBLK)
    out = lax.map(lambda xs: fox_block(xs[0], xs[1], xs[2], k, v, c, pos), (qb, cb, pb))
    return out.transpose(1, 0, 2, 3, 4).reshape(bsz, t_, FOX_HEADS, HEAD_DIM)


def fox_sample(q, k_new, v_new, logf_new, k_cache, v_cache, logf_cache):
    past = k_cache.shape[1]
    t_ = q.shape[1]
    k = jnp.concatenate([k_cache.astype(k_new.dtype), k_new], axis=1)
    v = jnp.concatenate([v_cache.astype(v_new.dtype), v_new], axis=1)
    c = jnp.cumsum(jnp.concatenate([logf_cache.astype(jnp.float32), logf_new], axis=1), axis=1)
    pos_k = jnp.arange(past + t_)
    pos_q = past + jnp.arange(t_)
    return fox_block(q, c[:, past:], pos_q, k, v, c, pos_k)


def gated_delta_rule(q, k, v, g, beta, s0):
    bsz, t_, nh, dk = q.shape
    dv = v.shape[-1]
    lc = min(t_, CHUNK)
    n = t_ // lc

    def chunks(x):
        return x.reshape(bsz, n, lc, nh, x.shape[-1]).transpose(1, 0, 3, 2, 4)

    qc = chunks(q) * (dk ** -0.5)
    kc = chunks(k)
    vc = chunks(v.astype(jnp.float32))
    gc = chunks(g[..., None])[..., 0]
    bc = chunks(beta[..., None])[..., 0]
    gcum = jnp.cumsum(gc, axis=-1)
    tril = jnp.tril(jnp.ones((lc, lc), dtype=bool))
    strict = jnp.tril(jnp.ones((lc, lc), dtype=bool), k=-1)
    decay = jnp.exp(jnp.where(tril, gcum[..., :, None] - gcum[..., None, :], -jnp.inf))
    kb = kc * bc[..., None]
    a = jnp.where(strict, jnp.einsum('nbhid,nbhjd->nbhij', kb, kc) * decay, 0.0)
    ia = a + jnp.eye(lc, dtype=jnp.float32)
    u = lax.linalg.triangular_solve(ia, vc * bc[..., None], left_side=True, lower=True, unit_diagonal=True)
    w = lax.linalg.triangular_solve(ia, kb * jnp.exp(gcum)[..., None], left_side=True, lower=True,
                                    unit_diagonal=True)
    qk = jnp.einsum('nbhid,nbhjd->nbhij', qc, kc) * decay

    def step(s, xs):
        q_i, k_i, u_i, w_i, g_i, qk_i = xs
        v_new = u_i - jnp.einsum('bhld,bhdv->bhlv', w_i, s)
        o_i = (jnp.einsum('bhld,bhdv->bhlv', q_i * jnp.exp(g_i)[..., None], s)
               + jnp.einsum('bhij,bhjv->bhiv', qk_i, v_new))
        g_last = g_i[..., -1:]
        s = s * jnp.exp(g_last)[..., None] + jnp.einsum(
            'bhld,bhlv->bhdv', k_i * jnp.exp(g_last - g_i)[..., None], v_new)
        return s, o_i

    s_fin, o = lax.scan(step, s0.astype(jnp.float32), (qc, kc, u, w, gcum, qk))
    o = o.transpose(1, 0, 3, 2, 4).reshape(bsz, t_, nh, dv)
    return o, s_fin


def gdn_mixer(q_raw, k_raw, v_raw, a_raw, b_raw, z_raw, conv_buf, s0, conv_w, a_log, dt_bias, norm_w):
    bsz, t_ = q_raw.shape[:2]
    qkv = jnp.concatenate([q_raw, k_raw, v_raw], axis=-1)
    ext = jnp.concatenate([conv_buf.astype(qkv.dtype), qkv], axis=1)
    new_buf = ext[:, ext.shape[1] - (CONV_W - 1):]
    conv = ext[:, 0:t_] * conv_w[0]
    for i in range(1, CONV_W):
        conv = conv + ext[:, i:i + t_] * conv_w[i]
    conv = jax.nn.silu(conv)
    q = l2_normalize(conv[..., :GDN_QK_WIDTH].reshape(bsz, t_, GDN_HEADS, GDN_DK))
    k = l2_normalize(conv[..., GDN_QK_WIDTH:2 * GDN_QK_WIDTH].reshape(bsz, t_, GDN_HEADS, GDN_DK))
    v = conv[..., 2 * GDN_QK_WIDTH:].reshape(bsz, t_, GDN_HEADS, GDN_DV)
    g = -jnp.exp(a_log.astype(jnp.float32)) * jax.nn.softplus(
        a_raw.astype(jnp.float32) + dt_bias.astype(jnp.float32))
    beta = jax.nn.sigmoid(b_raw.astype(jnp.float32))
    o, s_fin = gated_delta_rule(q, k, v, g, beta, s0)
    z = z_raw.reshape(bsz, t_, GDN_HEADS, GDN_DV).astype(jnp.float32)
    o = rms_norm(o, norm_w) * jax.nn.silu(z)
    return o.reshape(bsz, t_, GDN_V_WIDTH).astype(q_raw.dtype), s_fin, new_buf


def mixer(h, w_in_l, fox_f_bias_l, fox_norm_w_l, gdn_conv_w_l, gdn_a_log_l, gdn_dt_bias_l, gdn_norm_w_l,
          w_out_l, fox_cache, conv_buf, s0):
    bsz, t_ = h.shape[:2]
    proj = h @ w_in_l
    parts = []
    start = 0
    for width in IN_SPLITS:
        parts.append(proj[..., start:start + width])
        start += width
    fq, fk, fv, ff, gq, gk, gv, ga, gb, gz = parts
    fq = fq.reshape(bsz, t_, FOX_HEADS, HEAD_DIM)
    fk = fk.reshape(bsz, t_, FOX_HEADS, HEAD_DIM)
    fv = fv.reshape(bsz, t_, FOX_HEADS, HEAD_DIM)
    logf = jax.nn.log_sigmoid(ff.astype(jnp.float32) + fox_f_bias_l.astype(jnp.float32))
    if fox_cache is None:
        fo = fox_prompt(fq, fk, fv, logf)
    else:
        fo = fox_sample(fq, fk, fv, logf, fox_cache[0], fox_cache[1], fox_cache[2])
    fo = rms_norm(fo, fox_norm_w_l).reshape(bsz, t_, FOX_WIDTH)
    go, s_fin, new_buf = gdn_mixer(gq, gk, gv, ga, gb, gz, conv_buf, s0,
                                   gdn_conv_w_l, gdn_a_log_l, gdn_dt_bias_l, gdn_norm_w_l)
    out = jnp.concatenate([fo, go.astype(fo.dtype)], axis=-1) @ w_out_l
    return out, (fk, fv, logf, s_fin, new_buf)


def swiglu(x, w_gate, w_up, w_down):
    return (jax.nn.silu(x @ w_gate) * (x @ w_up)) @ w_down


def setup_inputs(seed: int = 0) -> dict:
    key = jax.random.key(seed)
    ks = jax.random.split(key, 32)
    f32 = jnp.float32

    def nrm(k, shape, scale):
        return jax.random.normal(k, shape, f32) * scale

    dt = jnp.exp(jax.random.uniform(ks[14], (DEPTH, GDN_HEADS), f32, math.log(DT_MIN), math.log(DT_MAX)))
    return {
        'x_prompt': nrm(ks[0], (BATCH, SEQ, D_MODEL), 1.0),
        'x_sample': nrm(ks[1], (DEC_BATCH, DEC_SEQ, D_MODEL), 1.0),
        'cache_fox_k': nrm(ks[2], (DEPTH, DEC_BATCH, PAST_LEN, FOX_HEADS, HEAD_DIM), 1.0),
        'cache_fox_v': nrm(ks[3], (DEPTH, DEC_BATCH, PAST_LEN, FOX_HEADS, HEAD_DIM), 1.0),
        'cache_fox_logf': jax.nn.log_sigmoid(nrm(ks[4], (DEPTH, DEC_BATCH, PAST_LEN, FOX_HEADS), 1.0)
                                             + FGATE_BIAS_MEAN),
        'state_gdn': nrm(ks[5], (DEPTH, DEC_BATCH, GDN_HEADS, GDN_DK, GDN_DV), 0.1),
        'state_gdn_conv': nrm(ks[6], (DEPTH, DEC_BATCH, CONV_W - 1, GDN_CONV_CH), 1.0),
        'ln_in_g': 1.0 + nrm(ks[7], (D_MODEL,), 0.05),
        'ln_in_b': nrm(ks[8], (D_MODEL,), 0.02),
        'w_in': nrm(ks[9], (DEPTH, D_MODEL, IN_COLS), D_MODEL ** -0.5),
        'fox_f_bias': FGATE_BIAS_MEAN + nrm(ks[10], (DEPTH, FOX_HEADS), 0.1),
        'fox_norm_w': 1.0 + nrm(ks[11], (DEPTH, HEAD_DIM), 0.05),
        'gdn_conv_w': nrm(ks[12], (DEPTH, CONV_W, GDN_CONV_CH), CONV_W ** -0.5),
        'gdn_a_log': jnp.log(jax.random.uniform(ks[13], (DEPTH, GDN_HEADS), f32, 1.0, 16.0)),
        'gdn_dt_bias': dt + jnp.log(-jnp.expm1(-dt)),
        'gdn_norm_w': 1.0 + nrm(ks[15], (DEPTH, GDN_DV), 0.05),
        'w_out': nrm(ks[16], (DEPTH, MIX_WIDTH, D_MODEL), MIX_WIDTH ** -0.5 * DEEPNORM_BETA),
        'ln_mix_g': 1.0 + nrm(ks[17], (DEPTH, D_MODEL), 0.05),
        'ln_mix_b': nrm(ks[18], (DEPTH, D_MODEL), 0.02),
        'ffn_w_gate': nrm(ks[19], (DEPTH, D_MODEL, D_FF), D_MODEL ** -0.5),
        'ffn_w_up': nrm(ks[20], (DEPTH, D_MODEL, D_FF), D_MODEL ** -0.5),
        'ffn_w_down': nrm(ks[21], (DEPTH, D_FF, D_MODEL), D_FF ** -0.5 * DEEPNORM_BETA),
        'ln_ffn_g': 1.0 + nrm(ks[22], (DEPTH, D_MODEL), 0.05),
        'ln_ffn_b': nrm(ks[23], (DEPTH, D_MODEL), 0.02),
    }


def reference(x_prompt, x_sample, cache_fox_k, cache_fox_v, cache_fox_logf, state_gdn, state_gdn_conv,
              ln_in_g, ln_in_b, w_in, fox_f_bias, fox_norm_w, gdn_conv_w, gdn_a_log, gdn_dt_bias, gdn_norm_w,
              w_out, ln_mix_g, ln_mix_b, ffn_w_gate, ffn_w_up, ffn_w_down, ln_ffn_g, ln_ffn_b):
    xp = layer_norm(x_prompt, ln_in_g, ln_in_b)
    xs = layer_norm(x_sample, ln_in_g, ln_in_b)
    ps = []
    ss = []
    for l in range(DEPTH):
        lw = (w_in[l], fox_f_bias[l], fox_norm_w[l], gdn_conv_w[l], gdn_a_log[l], gdn_dt_bias[l],
              gdn_norm_w[l], w_out[l])
        bp = xp.shape[0]
        mp, st_p = mixer(xp, *lw, None,
                         jnp.zeros((bp, CONV_W - 1, GDN_CONV_CH), xp.dtype),
                         jnp.zeros((bp, GDN_HEADS, GDN_DK, GDN_DV), jnp.float32))
        ms, st_s = mixer(xs, *lw, (cache_fox_k[l], cache_fox_v[l], cache_fox_logf[l]),
                         state_gdn_conv[l], state_gdn[l])
        xp = layer_norm(DEEPNORM_ALPHA * xp + mp, ln_mix_g[l], ln_mix_b[l])
        xs = layer_norm(DEEPNORM_ALPHA * xs + ms, ln_mix_g[l], ln_mix_b[l])
        xp = layer_norm(DEEPNORM_ALPHA * xp + swiglu(xp, ffn_w_gate[l], ffn_w_up[l], ffn_w_down[l]),
                        ln_ffn_g[l], ln_ffn_b[l])
        xs = layer_norm(DEEPNORM_ALPHA * xs + swiglu(xs, ffn_w_gate[l], ffn_w_up[l], ffn_w_down[l]),
                        ln_ffn_g[l], ln_ffn_b[l])
        ps.append(st_p)
        ss.append(st_s)

    def stacked(states, i):
        return jnp.stack([st[i] for st in states], axis=0)

    return (xp, xs,
            stacked(ps, 0), stacked(ps, 1), stacked(ps, 2), stacked(ps, 3), stacked(ps, 4),
            stacked(ss, 0), stacked(ss, 1), stacked(ss, 2), stacked(ss, 3), stacked(ss, 4))
```

```python
import functools
import math

import jax
import jax.numpy as jnp
from jax import lax
from jax.experimental import pallas as pl
from jax.experimental.pallas import tpu as pltpu

LN_EPS = 1e-5
RMS_EPS = 1e-6
L2_EPS = 1e-6
CHUNK = 64
CONV_W = 4
HD = 128
LANES = 128
VMEM_LIMIT = 56 * 1024 * 1024

F32 = jnp.float32
BF16 = jnp.bfloat16


def _pick(dim, cands):
    for c in cands:
        if c <= dim and dim % c == 0:
            return c
    return dim


def _params(sem):
    return pltpu.CompilerParams(dimension_semantics=sem, vmem_limit_bytes=VMEM_LIMIT)


def _split3(x):
    hi = x.astype(BF16)
    r1 = x - hi.astype(F32)
    mid = r1.astype(BF16)
    lo = (r1 - mid.astype(F32)).astype(BF16)
    return hi, mid, lo


def _dot(a, b):
    return jnp.dot(a, b, preferred_element_type=F32)


def _dot_nt(a, b):
    return lax.dot_general(a, b, (((1,), (1,)), ((), ())), preferred_element_type=F32)


def _dot_tn(a, b):
    return lax.dot_general(a, b, (((0,), (0,)), ((), ())), preferred_element_type=F32)


def _ln_rows(y, g, b):
    mu = jnp.mean(y, axis=-1, keepdims=True)
    d = y - mu
    var = jnp.mean(d * d, axis=-1, keepdims=True)
    return d * lax.rsqrt(var + LN_EPS) * g + b


def _silu(x):
    return x / (1.0 + jnp.exp(-x))


def _ln_kernel(x_ref, g_ref, b_ref, o_ref, ob_ref):
    y = _ln_rows(x_ref[...], g_ref[...], b_ref[...])
    o_ref[...] = y
    ob_ref[...] = y.astype(BF16)


def layer_norm_rows(x, g, b):
    m, d = x.shape
    tm = _pick(m, (512, 256, 128, 64, 32, 16, 8))
    return pl.pallas_call(
        _ln_kernel,
        grid=(m // tm,),
        in_specs=[pl.BlockSpec((tm, d), lambda i: (i, 0)),
                  pl.BlockSpec((1, d), lambda i: (0, 0)),
                  pl.BlockSpec((1, d), lambda i: (0, 0))],
        out_specs=[pl.BlockSpec((tm, d), lambda i: (i, 0)),
                   pl.BlockSpec((tm, d), lambda i: (i, 0))],
        out_shape=[jax.ShapeDtypeStruct((m, d), F32), jax.ShapeDtypeStruct((m, d), BF16)],
        compiler_params=_params(("parallel",)),
        name="ln_in",
    )(x, g.reshape(1, d), b.reshape(1, d))


def _mm_kernel(x_ref, w_ref, *o_refs):
    acc = _dot(x_ref[...], w_ref[...])
    for o in o_refs:
        o[...] = acc.astype(o.dtype)


def matmul(x, w, out_dtypes):
    m, k = x.shape
    n = w.shape[1]
    tm = _pick(m, (1024, 512, 256, 128, 64, 32, 16, 8))
    tn = _pick(n, (1024, 512, 256, 128))
    return pl.pallas_call(
        _mm_kernel,
        grid=(n // tn, m // tm),
        in_specs=[pl.BlockSpec((tm, k), lambda j, i: (i, 0)),
                  pl.BlockSpec((k, tn), lambda j, i: (0, j))],
        out_specs=[pl.BlockSpec((tm, tn), lambda j, i: (i, j)) for _ in out_dtypes],
        out_shape=[jax.ShapeDtypeStruct((m, n), dt) for dt in out_dtypes],
        compiler_params=_params(("parallel", "parallel")),
        name="proj",
    )(x, w)


def _gates_kernel(x_ref, w_ref, bias_ref, alog_ref, logf_ref, col_ref, row_ref, carry_ref, *,
                  sb, chunk, nh):
    t = pl.program_id(1)
    tb = x_ref.shape[0]

    @pl.when(t == 0)
    def _():
        carry_ref[...] = jnp.zeros_like(carry_ref)

    raw = _dot(x_ref[...], w_ref[...]) + bias_ref[...]
    lane = lax.broadcasted_iota(jnp.int32, (1, LANES), 1)
    sp_neg = jnp.maximum(-raw, 0.0) + jnp.log1p(jnp.exp(-jnp.abs(raw)))
    sp_pos = jnp.maximum(raw, 0.0) + jnp.log1p(jnp.exp(-jnp.abs(raw)))
    logf = -sp_neg
    g = -jnp.exp(alog_ref[...]) * sp_pos
    beta = 1.0 / (1.0 + jnp.exp(-raw))
    is_f = lane < nh
    is_g = (lane >= nh) & (lane < 2 * nh)
    is_b = (lane >= 2 * nh) & (lane < 3 * nh)
    logf_ref[...] = logf[:, :nh]

    r = lax.broadcasted_iota(jnp.int32, (sb, sb), 0)
    c = lax.broadcasted_iota(jnp.int32, (sb, sb), 1)
    tri_full = (r >= c).astype(BF16)
    tri_chunk = ((r >= c) & ((r // chunk) == (c // chunk))).astype(BF16)
    er = lax.broadcasted_iota(jnp.int32, (32, LANES), 0)
    ec = lax.broadcasted_iota(jnp.int32, (32, LANES), 1)
    sel = (er == ec).astype(BF16)

    carry = carry_ref[...]
    for s in range(tb // sb):
        rows = slice(s * sb, (s + 1) * sb)
        lf = jnp.where(is_f, logf[rows], 0.0)
        gg = jnp.where(is_g, g[rows], 0.0)
        cs = jnp.zeros((sb, LANES), F32)
        for part in _split3(lf):
            cs = cs + _dot(tri_full, part)
        for part in _split3(gg):
            cs = cs + _dot(tri_chunk, part)
        cs = cs + carry
        carry = jnp.where(is_f, cs[sb - 1:sb, :], 0.0)
        colv = jnp.where(is_b, beta[rows], cs)
        col_ref[rows, :] = colv
        rv = jnp.zeros((32, sb), F32)
        for part in _split3(colv):
            rv = rv + _dot_nt(sel, part)
        row_ref[:, rows] = rv
    carry_ref[...] = carry


def gates(xb, w_small, bias_row, alog_row, bsz, t_len, chunk, nh):
    m, d = xb.shape
    tb = _pick(t_len, (512, 256, 128, 64, 32, 16))
    sb = min(tb, 128)
    nt = t_len // tb
    kern = functools.partial(_gates_kernel, sb=sb, chunk=chunk, nh=nh)
    return pl.pallas_call(
        kern,
        grid=(bsz, nt),
        in_specs=[pl.BlockSpec((tb, d), lambda b, t: (b * nt + t, 0)),
                  pl.BlockSpec((d, LANES), lambda b, t: (0, 0)),
                  pl.BlockSpec((1, LANES), lambda b, t: (0, 0)),
                  pl.BlockSpec((1, LANES), lambda b, t: (0, 0))],
        out_specs=[pl.BlockSpec((None, tb, nh), lambda b, t: (b, t, 0)),
                   pl.BlockSpec((None, tb, LANES), lambda b, t: (b, t, 0)),
                   pl.BlockSpec((None, 32, tb), lambda b, t: (b, 0, t))],
        out_shape=[jax.ShapeDtypeStruct((bsz, t_len, nh), F32),
                   jax.ShapeDtypeStruct((bsz, t_len, LANES), F32),
                   jax.ShapeDtypeStruct((bsz, 32, t_len), F32)],
        scratch_shapes=[pltpu.VMEM((1, LANES), F32)],
        compiler_params=_params(("parallel", "arbitrary")),
        name="gates",
    )(xb, w_small, bias_row, alog_row)


def _fox_prompt_kernel(q_ref, k_ref, v_ref, col_ref, row_ref, nw_ref, o_ref, *, nh, tq):
    qi = pl.program_id(1)
    scale = HD ** -0.5
    nw = nw_ref[...]
    r = lax.broadcasted_iota(jnp.int32, (tq, tq), 0)
    c = lax.broadcasted_iota(jnp.int32, (tq, tq), 1)
    causal = r >= c
    for h in range(nh):
        cols = slice(h * HD, (h + 1) * HD)
        q = q_ref[:, cols]
        cq = col_ref[:, h:h + 1]

        def block(j, masked):
            start = pl.multiple_of(j * tq, tq)
            kb = k_ref[pl.ds(start, tq), cols]
            vb = v_ref[pl.ds(start, tq), cols]
            ck = row_ref[h:h + 1, pl.ds(start, tq)]
            s = _dot_nt(q, kb) * scale + (cq - ck)
            if masked:
                s = jnp.where(causal, s, -jnp.inf)
            return s, vb

        def step(j, carry):
            m, l, acc = carry
            s, vb = block(j, False)
            m_new = jnp.maximum(m, jnp.max(s, axis=-1, keepdims=True))
            a = jnp.exp(m - m_new)
            p = jnp.exp(s - m_new)
            l = a * l + jnp.sum(p, axis=-1, keepdims=True)
            acc = a * acc + _dot(p.astype(BF16), vb)
            return m_new, l, acc

        s, vb = block(qi, True)
        m0 = jnp.max(s, axis=-1, keepdims=True)
        p = jnp.exp(s - m0)
        l0 = jnp.sum(p, axis=-1, keepdims=True)
        acc0 = _dot(p.astype(BF16), vb)
        m, l, acc = lax.fori_loop(0, qi, step, (m0, l0, acc0))
        o = acc / l
        o = o * lax.rsqrt(jnp.mean(o * o, axis=-1, keepdims=True) + RMS_EPS) * nw
        o_ref[:, cols] = o.astype(o_ref.dtype)


def fox_prompt(qb, kb, vb, col, row, norm_w, bsz, t_len, nh):
    tq = _pick(t_len, (256, 128, 64, 32, 16))
    nq = t_len // tq
    w = nh * HD
    kern = functools.partial(_fox_prompt_kernel, nh=nh, tq=tq)
    return pl.pallas_call(
        kern,
        grid=(bsz, nq),
        in_specs=[pl.BlockSpec((tq, w), lambda b, i: (b * nq + i, 0)),
                  pl.BlockSpec((t_len, w), lambda b, i: (b, 0)),
                  pl.BlockSpec((t_len, w), lambda b, i: (b, 0)),
                  pl.BlockSpec((None, tq, LANES), lambda b, i: (b, i, 0)),
                  pl.BlockSpec((None, 32, t_len), lambda b, i: (b, 0, 0)),
                  pl.BlockSpec((1, HD), lambda b, i: (0, 0))],
        out_specs=pl.BlockSpec((tq, w), lambda b, i: (b * nq + i, 0)),
        out_shape=jax.ShapeDtypeStruct((bsz * t_len, w), BF16),
        compiler_params=_params(("parallel", "parallel")),
        name="fox_prompt",
    )(qb, kb, vb, col, row, norm_w)


def _fox_sample_kernel(q_ref, kn_ref, vn_ref, kc_ref, vc_ref, lfc_ref, col_ref, row_ref, nw_ref,
                       o_ref, *, nh):
    t_len = q_ref.shape[0]
    past = kc_ref.shape[0]
    scale = HD ** -0.5
    nw = nw_ref[...]
    r = lax.broadcasted_iota(jnp.int32, (past, past), 0)
    c = lax.broadcasted_iota(jnp.int32, (past, past), 1)
    upper = (r <= c).astype(BF16)
    cc = jnp.zeros(lfc_ref.shape, F32)
    for part in _split3(lfc_ref[...]):
        cc = cc + _dot(part, upper)
    rr = lax.broadcasted_iota(jnp.int32, (t_len, t_len), 0)
    rc = lax.broadcasted_iota(jnp.int32, (t_len, t_len), 1)
    causal = rr >= rc
    for h in range(nh):
        cols = slice(h * HD, (h + 1) * HD)
        q = q_ref[:, cols]
        cn_col = col_ref[:, h:h + 1]
        cn_row = row_ref[h:h + 1, :]
        cch = cc[h:h + 1, :]
        tot = cch[:, past - 1:past]
        kc = kc_ref[:, cols].astype(BF16)
        vc = vc_ref[:, cols].astype(BF16)
        s_c = _dot_nt(q, kc) * scale + (cn_col + (tot - cch))
        s_n = _dot_nt(q, kn_ref[:, cols]) * scale + (cn_col - cn_row)
        s_n = jnp.where(causal, s_n, -jnp.inf)
        m = jnp.maximum(jnp.max(s_c, axis=-1, keepdims=True), jnp.max(s_n, axis=-1, keepdims=True))
        p_c = jnp.exp(s_c - m)
        p_n = jnp.exp(s_n - m)
        l = jnp.sum(p_c, axis=-1, keepdims=True) + jnp.sum(p_n, axis=-1, keepdims=True)
        o = (_dot(p_c.astype(BF16), vc) + _dot(p_n.astype(BF16), vn_ref[:, cols])) / l
        o = o * lax.rsqrt(jnp.mean(o * o, axis=-1, keepdims=True) + RMS_EPS) * nw
        o_ref[:, cols] = o.astype(o_ref.dtype)


def fox_sample(qb, knb, vnb, k_cache, v_cache, lf_cache_t, col, row, norm_w, bsz, t_len, nh):
    w = nh * HD
    past = k_cache.shape[1]
    kern = functools.partial(_fox_sample_kernel, nh=nh)
    return pl.pallas_call(
        kern,
        grid=(bsz,),
        in_specs=[pl.BlockSpec((t_len, w), lambda b: (b, 0)),
                  pl.BlockSpec((t_len, w), lambda b: (b, 0)),
                  pl.BlockSpec((t_len, w), lambda b: (b, 0)),
                  pl.BlockSpec((None, past, w), lambda b: (b, 0, 0)),
                  pl.BlockSpec((None, past, w), lambda b: (b, 0, 0)),
                  pl.BlockSpec((None, nh, past), lambda b: (b, 0, 0)),
                  pl.BlockSpec((None, t_len, LANES), lambda b: (b, 0, 0)),
                  pl.BlockSpec((None, 32, t_len), lambda b: (b, 0, 0)),
                  pl.BlockSpec((1, HD), lambda b: (0, 0))],
        out_specs=pl.BlockSpec((t_len, w), lambda b: (b, 0)),
        out_shape=jax.ShapeDtypeStruct((bsz * t_len, w), BF16),
        compiler_params=_params(("parallel",)),
        name="fox_sample",
    )(qb, knb, vnb, k_cache, v_cache, lf_cache_t, col, row, norm_w)


def _gdn_prep_kernel(raw_ref, prev_ref, cbuf_ref, cw_ref, col_ref, row_ref,
                     u_ref, w_ref, qg_ref, kd_ref, qk_ref, ext_ref, *, nh, lc):
    t = pl.program_id(1)
    tb = raw_ref.shape[0]
    width = nh * HD
    pad = 8
    @pl.when(t == 0)
    def _():
        ext_ref[0:pad, :] = cbuf_ref[...]

    @pl.when(t > 0)
    def _():
        ext_ref[0:pad, :] = prev_ref[...]

    ext_ref[pad:pad + tb, :] = raw_ref[...]

    r = lax.broadcasted_iota(jnp.int32, (lc, lc), 0)
    c = lax.broadcasted_iota(jnp.int32, (lc, lc), 1)
    tril = r >= c
    strict = r > c
    eye = (r == c).astype(F32)
    n_sq = int(math.log2(lc)) - 1
    qscale = HD ** -0.5
    zpad = jnp.zeros((lc, HD - lc), BF16) if lc < HD else None

    def conv_act(seg, h):
        lo = seg * width + h * HD
        acc = None
        for i in range(CONV_W):
            term = ext_ref[pl.ds(pad - (CONV_W - 1) + i, tb), lo:lo + HD] * cw_ref[i:i + 1, lo:lo + HD]
            acc = term if acc is None else acc + term
        return _silu(acc)

    def l2n(x):
        return x * lax.rsqrt(jnp.sum(x * x, axis=-1, keepdims=True) + L2_EPS)

    for h in range(nh):
        cols = slice(h * HD, (h + 1) * HD)
        q_all = l2n(conv_act(0, h))
        k_all = l2n(conv_act(1, h))
        v_all = conv_act(2, h)
        for ci in range(tb // lc):
            rows = slice(ci * lc, (ci + 1) * lc)
            q = q_all[rows] * qscale
            k = k_all[rows]
            v = v_all[rows]
            g_col = col_ref[rows, nh + h:nh + h + 1]
            b_col = col_ref[rows, 2 * nh + h:2 * nh + h + 1]
            g_row = row_ref[nh + h:nh + h + 1, rows]
            g_last = g_row[:, lc - 1:lc]
            decay = jnp.exp(jnp.where(tril, g_col - g_row, -jnp.inf))
            kb = k * b_col
            kbf = k.astype(BF16)
            a = jnp.where(strict, _dot_nt(kb.astype(BF16), kbf) * decay, 0.0)
            p = -a
            tinv = eye + p
            for _ in range(n_sq):
                pb = p.astype(BF16)
                p = _dot(pb, pb)
                tinv = tinv + _dot(tinv.astype(BF16), p.astype(BF16))
            rhs = jnp.concatenate([v * b_col, kb * jnp.exp(g_col)], axis=-1)
            uw = _dot(tinv.astype(BF16), rhs.astype(BF16))
            qk = _dot_nt(q.astype(BF16), kbf) * decay
            u_ref[rows, cols] = uw[:, :HD]
            w_ref[rows, cols] = uw[:, HD:].astype(BF16)
            qg_ref[rows, cols] = (q * jnp.exp(g_col)).astype(BF16)
            kd_ref[rows, cols] = (k * jnp.exp(g_last - g_col)).astype(BF16)
            qkb = qk.astype(BF16)
            if zpad is not None:
                qkb = jnp.concatenate([qkb, zpad], axis=-1)
            qk_ref[rows, cols] = qkb


def gdn_prep(raw, conv_state8, conv_w8, col, row, bsz, t_len, nh, lc):
    m, c3 = raw.shape
    width = nh * HD
    tb = _pick(t_len, (128, 64, 32, 16))
    nt = t_len // tb
    kern = functools.partial(_gdn_prep_kernel, nh=nh, lc=lc)
    blk = lambda b, t: (b * nt + t, 0)
    return pl.pallas_call(
        kern,
        grid=(bsz, nt),
        in_specs=[pl.BlockSpec((tb, c3), blk),
                  pl.BlockSpec((8, c3), lambda b, t: (jnp.maximum((b * nt + t) * (tb // 8) - 1, 0), 0)),
                  pl.BlockSpec((None, 8, c3), lambda b, t: (b, 0, 0)),
                  pl.BlockSpec((8, c3), lambda b, t: (0, 0)),
                  pl.BlockSpec((None, tb, LANES), lambda b, t: (b, t, 0)),
                  pl.BlockSpec((None, 32, tb), lambda b, t: (b, 0, t))],
        out_specs=[pl.BlockSpec((tb, width), blk) for _ in range(5)],
        out_shape=[jax.ShapeDtypeStruct((m, width), F32)] +
                  [jax.ShapeDtypeStruct((m, width), BF16) for _ in range(4)],
        scratch_shapes=[pltpu.VMEM((tb + 8, c3), F32)],
        compiler_params=_params(("parallel", "parallel")),
        name="gdn_prep",
    )(raw, raw, conv_state8, conv_w8, col, row)


def _gdn_scan_kernel(u_ref, w_ref, qg_ref, kd_ref, qk_ref, z_ref, row_ref, nw_ref, s0_ref,
                     o_ref, s_ref, *, nh, lc):
    t = pl.program_id(1)
    tb = u_ref.shape[0]
    nw = nw_ref[...]

    @pl.when(t == 0)
    def _():
        s_ref[...] = s0_ref[...]

    for ci in range(tb // lc):
        rows = slice(ci * lc, (ci + 1) * lc)
        for h in range(nh):
            cols = slice(h * HD, (h + 1) * HD)
            s = s_ref[h]
            sb = s.astype(BF16)
            ws = _dot(w_ref[rows, cols], sb)
            v_new = u_ref[rows, cols] - ws
            vb = v_new.astype(BF16)
            o = _dot(qg_ref[rows, cols], sb) + _dot(qk_ref[rows, h * HD:h * HD + lc], vb)
            g_last = row_ref[nh + h:nh + h + 1, ci * lc + lc - 1:ci * lc + lc]
            s_ref[h] = s * jnp.exp(g_last) + _dot_tn(kd_ref[rows, cols], vb)
            z = z_ref[rows, cols]
            o = o * lax.rsqrt(jnp.mean(o * o, axis=-1, keepdims=True) + RMS_EPS) * nw * _silu(z)
            o_ref[rows, cols] = o.astype(o_ref.dtype)


def gdn_scan(u, w, qg, kd, qk, z, row, norm_w, s0, bsz, t_len, nh, lc):
    m, width = u.shape
    tb = _pick(t_len, (256, 128, 64, 32, 16))
    nt = t_len // tb
    kern = functools.partial(_gdn_scan_kernel, nh=nh, lc=lc)
    blk = lambda b, t: (b * nt + t, 0)
    return pl.pallas_call(
        kern,
        grid=(bsz, nt),
        in_specs=[pl.BlockSpec((tb, width), blk) for _ in range(6)] +
                 [pl.BlockSpec((None, 32, tb), lambda b, t: (b, 0, t)),
                  pl.BlockSpec((1, HD), lambda b, t: (0, 0)),
                  pl.BlockSpec((None, nh, HD, HD), lambda b, t: (b, 0, 0, 0))],
        out_specs=[pl.BlockSpec((tb, width), blk),
                   pl.BlockSpec((None, nh, HD, HD), lambda b, t: (b, 0, 0, 0))],
        out_shape=[jax.ShapeDtypeStruct((m, width), BF16),
                   jax.ShapeDtypeStruct((bsz, nh, HD, HD), F32)],
        compiler_params=_params(("parallel", "arbitrary")),
        name="gdn_scan",
    )(u, w, qg, kd, qk, z, row, norm_w, s0)


def _out_ln_kernel(a_ref, b_ref, wa_ref, wb_ref, res_ref, g_ref, beta_ref, o_ref, ob_ref, *, alpha):
    y = _dot(a_ref[...], wa_ref[...]) + _dot(b_ref[...], wb_ref[...])
    y = _ln_rows(alpha * res_ref[...] + y, g_ref[...], beta_ref[...])
    o_ref[...] = y
    ob_ref[...] = y.astype(BF16)


def out_proj_ln(mix_a, mix_b, w_a, w_b, res, g, beta, alpha):
    m, ka = mix_a.shape
    kb = mix_b.shape[1]
    d = w_a.shape[1]
    tm = _pick(m, (512, 256, 128, 64, 32, 16, 8))
    kern = functools.partial(_out_ln_kernel, alpha=alpha)
    row = lambda i: (i, 0)
    fixed = lambda i: (0, 0)
    return pl.pallas_call(
        kern,
        grid=(m // tm,),
        in_specs=[pl.BlockSpec((tm, ka), row), pl.BlockSpec((tm, kb), row),
                  pl.BlockSpec((ka, d), fixed), pl.BlockSpec((kb, d), fixed),
                  pl.BlockSpec((tm, d), row),
                  pl.BlockSpec((1, d), fixed), pl.BlockSpec((1, d), fixed)],
        out_specs=[pl.BlockSpec((tm, d), row), pl.BlockSpec((tm, d), row)],
        out_shape=[jax.ShapeDtypeStruct((m, d), F32), jax.ShapeDtypeStruct((m, d), BF16)],
        compiler_params=_params(("parallel",)),
        name="out_proj_ln",
    )(mix_a, mix_b, w_a, w_b, res, g.reshape(1, d), beta.reshape(1, d))


def _ffn_up_kernel(x_ref, wg_ref, wu_ref, h_ref):
    x = x_ref[...]
    gate = _dot(x, wg_ref[...])
    up = _dot(x, wu_ref[...])
    h_ref[...] = (_silu(gate) * up).astype(h_ref.dtype)


def ffn_up(xb, wg, wu):
    m, d = xb.shape
    f = wg.shape[1]
    tm = _pick(m, (512, 256, 128, 64, 32, 16, 8))
    tf = _pick(f, (1408, 1024, 512, 256, 128))
    return pl.pallas_call(
        _ffn_up_kernel,
        grid=(f // tf, m // tm),
        in_specs=[pl.BlockSpec((tm, d), lambda j, i: (i, 0)),
                  pl.BlockSpec((d, tf), lambda j, i: (0, j)),
                  pl.BlockSpec((d, tf), lambda j, i: (0, j))],
        out_specs=pl.BlockSpec((tm, tf), lambda j, i: (i, j)),
        out_shape=jax.ShapeDtypeStruct((m, f), BF16),
        compiler_params=_params(("parallel", "parallel")),
        name="ffn_up",
    )(xb, wg, wu)


def _ffn_down_kernel(h_ref, w_ref, res_ref, g_ref, beta_ref, o_ref, ob_ref, acc_ref, *, alpha):
    k = pl.program_id(1)

    @pl.when(k == 0)
    def _():
        acc_ref[...] = jnp.zeros_like(acc_ref)

    acc_ref[...] += _dot(h_ref[...], w_ref[...])

    @pl.when(k == pl.num_programs(1) - 1)
    def _():
        y = _ln_rows(alpha * res_ref[...] + acc_ref[...], g_ref[...], beta_ref[...])
        o_ref[...] = y
        ob_ref[...] = y.astype(BF16)


def ffn_down_ln(h, w, res, g, beta, alpha):
    m, f = h.shape
    d = w.shape[1]
    tm = _pick(m, (512, 256, 128, 64, 32, 16, 8))
    tk = _pick(f, (1408, 1024, 512, 256, 128))
    kern = functools.partial(_ffn_down_kernel, alpha=alpha)
    return pl.pallas_call(
        kern,
        grid=(m // tm, f // tk),
        in_specs=[pl.BlockSpec((tm, tk), lambda i, k: (i, k)),
                  pl.BlockSpec((tk, d), lambda i, k: (k, 0)),
                  pl.BlockSpec((tm, d), lambda i, k: (i, 0)),
                  pl.BlockSpec((1, d), lambda i, k: (0, 0)),
                  pl.BlockSpec((1, d), lambda i, k: (0, 0))],
        out_specs=[pl.BlockSpec((tm, d), lambda i, k: (i, 0)),
                   pl.BlockSpec((tm, d), lambda i, k: (i, 0))],
        out_shape=[jax.ShapeDtypeStruct((m, d), F32), jax.ShapeDtypeStruct((m, d), BF16)],
        scratch_shapes=[pltpu.VMEM((tm, d), F32)],
        compiler_params=_params(("parallel", "arbitrary")),
        name="ffn_down_ln",
    )(h, w, res, g.reshape(1, d), beta.reshape(1, d))


def _layer(x, xb, bsz, t_len, lw, fox_cache, conv_state, s0, alpha):
    nh = lw["nh"]
    width = nh * HD
    lc = min(t_len, CHUNK)
    qb, = matmul(xb, lw["w_fq"], (BF16,))
    kf, kb = matmul(xb, lw["w_fk"], (F32, BF16))
    vf, vb = matmul(xb, lw["w_fv"], (F32, BF16))
    raw, = matmul(xb, lw["w_gqkv"], (F32,))
    z, = matmul(xb, lw["w_gz"], (F32,))
    logf, col, row = gates(xb, lw["w_small"], lw["gate_bias"], lw["alog_row"], bsz, t_len, lc, nh)

    if fox_cache is None:
        mix_f = fox_prompt(qb, kb, vb, col, row, lw["fox_norm_w"], bsz, t_len, nh)
    else:
        k_cache, v_cache, lf_cache = fox_cache
        past = k_cache.shape[1]
        mix_f = fox_sample(qb, kb, vb, k_cache.reshape(bsz, past, width), v_cache.reshape(bsz, past, width),
                           jnp.transpose(lf_cache, (0, 2, 1)), col, row, lw["fox_norm_w"], bsz, t_len, nh)

    conv_state8 = jnp.pad(conv_state, ((0, 0), (8 - (CONV_W - 1), 0), (0, 0)))
    u, w, qg, kd, qk = gdn_prep(raw, conv_state8, lw["conv_w8"], col, row, bsz, t_len, nh, lc)
    mix_g, s_fin = gdn_scan(u, w, qg, kd, qk, z, row, lw["gdn_norm_w"], s0, bsz, t_len, nh, lc)

    x1, x1b = out_proj_ln(mix_f, mix_g, lw["w_out_f"], lw["w_out_g"], x, lw["ln_mix_g"], lw["ln_mix_b"], alpha)
    hmid = ffn_up(x1b, lw["w_gate"], lw["w_up"])
    x2, x2b = ffn_down_ln(hmid, lw["w_down"], x1, lw["ln_ffn_g"], lw["ln_ffn_b"], alpha)

    new_buf = raw.reshape(bsz, t_len, -1)[:, t_len - (CONV_W - 1):, :]
    state = (kf.reshape(bsz, t_len, nh, HD), vf.reshape(bsz, t_len, nh, HD), logf, s_fin, new_buf)
    return x2, x2b, state


def kernel(x_prompt, x_sample, cache_fox_k, cache_fox_v, cache_fox_logf, state_gdn, state_gdn_conv, ln_in_g, ln_in_b, w_in, fox_f_bias, fox_norm_w, gdn_conv_w, gdn_a_log, gdn_dt_bias, gdn_norm_w, w_out, ln_mix_g, ln_mix_b, ffn_w_gate, ffn_w_up, ffn_w_down, ln_ffn_g, ln_ffn_b):
    depth = w_in.shape[0]
    bp, tp, d = x_prompt.shape
    bs, ts, _ = x_sample.shape
    nh = fox_f_bias.shape[1]
    assert fox_norm_w.shape[1] == HD and gdn_norm_w.shape[1] == HD and gdn_a_log.shape[1] == nh
    width = nh * HD
    assert w_in.shape[2] == 7 * width + 3 * nh and gdn_conv_w.shape[2] == 3 * width
    alpha = (2 * depth) ** 0.25

    xp, xpb = layer_norm_rows(x_prompt.reshape(bp * tp, d), ln_in_g, ln_in_b)
    xs, xsb = layer_norm_rows(x_sample.reshape(bs * ts, d), ln_in_g, ln_in_b)

    zero_conv = jnp.zeros((bp, CONV_W - 1, 3 * width), F32)
    zero_state = jnp.zeros((bp, nh, HD, HD), F32)
    ps, ss = [], []
    for l in range(depth):
        wl = w_in[l]
        o_ff = 3 * width
        o_gq = o_ff + nh
        o_ga = o_gq + 3 * width
        o_gz = o_ga + 2 * nh
        pad_lanes = LANES - 3 * nh
        small = jnp.concatenate([wl[:, o_ff:o_gq], wl[:, o_ga:o_gz], jnp.zeros((d, pad_lanes), F32)], axis=1)
        gate_bias = jnp.concatenate([fox_f_bias[l], gdn_dt_bias[l], jnp.zeros((LANES - 2 * nh,), F32)]).reshape(1, LANES)
        alog_row = jnp.concatenate([jnp.zeros((nh,), F32), gdn_a_log[l], jnp.zeros((LANES - 2 * nh,), F32)]).reshape(1, LANES)
        lw = dict(
            nh=nh,
            w_fq=wl[:, 0:width].astype(BF16),
            w_fk=wl[:, width:2 * width].astype(BF16),
            w_fv=wl[:, 2 * width:3 * width].astype(BF16),
            w_gqkv=wl[:, o_gq:o_ga].astype(BF16),
            w_gz=wl[:, o_gz:o_gz + width].astype(BF16),
            w_small=small.astype(BF16),
            gate_bias=gate_bias, alog_row=alog_row,
            fox_norm_w=fox_norm_w[l].reshape(1, HD),
            gdn_norm_w=gdn_norm_w[l].reshape(1, HD),
            conv_w8=jnp.pad(gdn_conv_w[l], ((0, 8 - CONV_W), (0, 0))),
            w_out_f=w_out[l, :width].astype(BF16),
            w_out_g=w_out[l, width:].astype(BF16),
            ln_mix_g=ln_mix_g[l], ln_mix_b=ln_mix_b[l],
            w_gate=ffn_w_gate[l].astype(BF16), w_up=ffn_w_up[l].astype(BF16),
            w_down=ffn_w_down[l].astype(BF16),
            ln_ffn_g=ln_ffn_g[l], ln_ffn_b=ln_ffn_b[l],
        )
        xp, xpb, st_p = _layer(xp, xpb, bp, tp, lw, None, zero_conv, zero_state, alpha)
        xs, xsb, st_s = _layer(xs, xsb, bs, ts, lw,
                               (cache_fox_k[l], cache_fox_v[l], cache_fox_logf[l]),
                               state_gdn_conv[l], state_gdn[l], alpha)
        ps.append(st_p)
        ss.append(st_s)

    def stacked(states, i):
        return jnp.stack([st[i] for st in states], axis=0)

    return (xp.reshape(bp, tp, d), xs.reshape(bs, ts, d),
            stacked(ps, 0), stacked(ps, 1), stacked(ps, 2), stacked(ps, 3), stacked(ps, 4),
            stacked(ss, 0), stacked(ss, 1), stacked(ss, 2), stacked(ss, 3), stacked(ss, 4))
```

```python
import functools
import math

import jax
import jax.numpy as jnp
from jax import lax
from jax.experimental import pallas as pl
from jax.experimental.pallas import tpu as pltpu

LN_EPS = 1e-5
RMS_EPS = 1e-6
L2_EPS = 1e-6
CHUNK = 64
CONV_W = 4
HD = 128
LANES = 128
VMEM_LIMIT = 56 * 1024 * 1024

F32 = jnp.float32
BF16 = jnp.bfloat16


def _pick(dim, cands):
    for c in cands:
        if c <= dim and dim % c == 0:
            return c
    return dim


def _params(sem):
    return pltpu.CompilerParams(dimension_semantics=sem, vmem_limit_bytes=VMEM_LIMIT)


def _split3(x):
    hi = x.astype(BF16)
    r1 = x - hi.astype(F32)
    mid = r1.astype(BF16)
    lo = (r1 - mid.astype(F32)).astype(BF16)
    return hi, mid, lo


def _dot(a, b):
    return jnp.dot(a, b, preferred_element_type=F32)


def _dot_nt(a, b):
    return lax.dot_general(a, b, (((1,), (1,)), ((), ())), preferred_element_type=F32)


def _dot_tn(a, b):
    return lax.dot_general(a, b, (((0,), (0,)), ((), ())), preferred_element_type=F32)


def _ln_rows(y, g, b):
    mu = jnp.mean(y, axis=-1, keepdims=True)
    d = y - mu
    var = jnp.mean(d * d, axis=-1, keepdims=True)
    return d * lax.rsqrt(var + LN_EPS) * g + b


def _silu(x):
    return x / (1.0 + jnp.exp(-x))


def _ln_kernel(x_ref, g_ref, b_ref, o_ref, ob_ref):
    y = _ln_rows(x_ref[...], g_ref[...], b_ref[...])
    o_ref[...] = y
    ob_ref[...] = y.astype(BF16)


def layer_norm_rows(x, g, b):
    m, d = x.shape
    tm = _pick(m, (512, 256, 128, 64, 32, 16, 8))
    return pl.pallas_call(
        _ln_kernel,
        grid=(m // tm,),
        in_specs=[pl.BlockSpec((tm, d), lambda i: (i, 0)),
                  pl.BlockSpec((1, d), lambda i: (0, 0)),
                  pl.BlockSpec((1, d), lambda i: (0, 0))],
        out_specs=[pl.BlockSpec((tm, d), lambda i: (i, 0)),
                   pl.BlockSpec((tm, d), lambda i: (i, 0))],
        out_shape=[jax.ShapeDtypeStruct((m, d), F32), jax.ShapeDtypeStruct((m, d), BF16)],
        compiler_params=_params(("parallel",)),
        name="ln_in",
    )(x, g.reshape(1, d), b.reshape(1, d))


def _mm_kernel(x_ref, w_ref, *o_refs):
    acc = _dot(x_ref[...], w_ref[...])
    for o in o_refs:
        o[...] = acc.astype(o.dtype)


def matmul(x, w, out_dtypes):
    m, k = x.shape
    n = w.shape[1]
    tm = _pick(m, (1024, 512, 256, 128, 64, 32, 16, 8))
    tn = _pick(n, (1024, 512, 256, 128))
    return pl.pallas_call(
        _mm_kernel,
        grid=(n // tn, m // tm),
        in_specs=[pl.BlockSpec((tm, k), lambda j, i: (i, 0)),
                  pl.BlockSpec((k, tn), lambda j, i: (0, j))],
        out_specs=[pl.BlockSpec((tm, tn), lambda j, i: (i, j)) for _ in out_dtypes],
        out_shape=[jax.ShapeDtypeStruct((m, n), dt) for dt in out_dtypes],
        compiler_params=_params(("parallel", "parallel")),
        name="proj",
    )(x, w)


def _gates_kernel(x_ref, w_ref, bias_ref, alog_ref, logf_ref, col_ref, row_ref, carry_ref, *,
                  sb, chunk, nh):
    t = pl.program_id(1)
    tb = x_ref.shape[0]

    @pl.when(t == 0)
    def _():
        carry_ref[...] = jnp.zeros_like(carry_ref)

    raw = _dot(x_ref[...], w_ref[...]) + bias_ref[...]
    lane = lax.broadcasted_iota(jnp.int32, (1, LANES), 1)
    sp_neg = jnp.maximum(-raw, 0.0) + jnp.log1p(jnp.exp(-jnp.abs(raw)))
    sp_pos = jnp.maximum(raw, 0.0) + jnp.log1p(jnp.exp(-jnp.abs(raw)))
    logf = -sp_neg
    g = -jnp.exp(alog_ref[...]) * sp_pos
    beta = 1.0 / (1.0 + jnp.exp(-raw))
    is_f = lane < nh
    is_g = (lane >= nh) & (lane < 2 * nh)
    is_b = (lane >= 2 * nh) & (lane < 3 * nh)
    logf_ref[...] = logf[:, :nh]

    r = lax.broadcasted_iota(jnp.int32, (sb, sb), 0)
    c = lax.broadcasted_iota(jnp.int32, (sb, sb), 1)
    tri_full = (r >= c).astype(BF16)
    tri_chunk = ((r >= c) & ((r // chunk) == (c // chunk))).astype(BF16)
    er = lax.broadcasted_iota(jnp.int32, (32, LANES), 0)
    ec = lax.broadcasted_iota(jnp.int32, (32, LANES), 1)
    sel = (er == ec).astype(BF16)

    carry = carry_ref[...]
    for s in range(tb // sb):
        rows = slice(s * sb, (s + 1) * sb)
        lf = jnp.where(is_f, logf[rows], 0.0)
        gg = jnp.where(is_g, g[rows], 0.0)
        cs = jnp.zeros((sb, LANES), F32)
        for part in _split3(lf):
            cs = cs + _dot(tri_full, part)
        for part in _split3(gg):
            cs = cs + _dot(tri_chunk, part)
        cs = cs + carry
        carry = jnp.where(is_f, cs[sb - 1:sb, :], 0.0)
        colv = jnp.where(is_b, beta[rows], cs)
        col_ref[rows, :] = colv
        rv = jnp.zeros((32, sb), F32)
        for part in _split3(colv):
            rv = rv + _dot_nt(sel, part)
        row_ref[:, rows] = rv
    carry_ref[...] = carry


def gates(xb, w_small, bias_row, alog_row, bsz, t_len, chunk, nh):
    m, d = xb.shape
    tb = _pick(t_len, (512, 256, 128, 64, 32, 16))
    sb = min(tb, 128)
    nt = t_len // tb
    kern = functools.partial(_gates_kernel, sb=sb, chunk=chunk, nh=nh)
    return pl.pallas_call(
        kern,
        grid=(bsz, nt),
        in_specs=[pl.BlockSpec((tb, d), lambda b, t: (b * nt + t, 0)),
                  pl.BlockSpec((d, LANES), lambda b, t: (0, 0)),
                  pl.BlockSpec((1, LANES), lambda b, t: (0, 0)),
                  pl.BlockSpec((1, LANES), lambda b, t: (0, 0))],
        out_specs=[pl.BlockSpec((None, tb, nh), lambda b, t: (b, t, 0)),
                   pl.BlockSpec((None, tb, LANES), lambda b, t: (b, t, 0)),
                   pl.BlockSpec((None, 32, tb), lambda b, t: (b, 0, t))],
        out_shape=[jax.ShapeDtypeStruct((bsz, t_len, nh), F32),
                   jax.ShapeDtypeStruct((bsz, t_len, LANES), F32),
                   jax.ShapeDtypeStruct((bsz, 32, t_len), F32)],
        scratch_shapes=[pltpu.VMEM((1, LANES), F32)],
        compiler_params=_params(("parallel", "arbitrary")),
        name="gates",
    )(xb, w_small, bias_row, alog_row)


def _fox_prompt_kernel(q_ref, k_ref, v_ref, col_ref, row_ref, nw_ref, o_ref, cq_ref, m_ref, acc_ref, *,
                       nh, tq):
    qi = pl.program_id(1)
    scale = HD ** -0.5
    nw = nw_ref[...]
    nl = tq // LANES
    r = lax.broadcasted_iota(jnp.int32, (tq, LANES), 0)
    c = lax.broadcasted_iota(jnp.int32, (tq, LANES), 1)
    ones = jnp.ones((tq, LANES), BF16)
    for h in range(nh):
        cq_ref[h] = jnp.broadcast_to(col_ref[:, h:h + 1], (tq, LANES))

    def scores(j, masked):
        start = pl.multiple_of(j * tq, tq)
        out = []
        for h in range(nh):
            cols = slice(h * HD, (h + 1) * HD)
            s = _dot_nt(q_ref[:, cols], k_ref[pl.ds(start, tq), cols])
            cq = cq_ref[h]
            pieces = []
            for k in range(nl):
                ck = row_ref[h:h + 1, pl.ds(start + k * LANES, LANES)]
                sk = s[:, k * LANES:(k + 1) * LANES] * scale + (cq - ck)
                if masked:
                    sk = jnp.where(r >= c + k * LANES, sk, -jnp.inf)
                pieces.append(sk)
            out.append(pieces)
        return start, out

    def row_max(pieces):
        mx = pieces[0]
        for sk in pieces[1:]:
            mx = jnp.maximum(mx, sk)
        return jnp.broadcast_to(jnp.max(mx, axis=-1, keepdims=True), (tq, LANES))

    def pv(p_pieces, start, h):
        p = jnp.concatenate(p_pieces, axis=-1).astype(BF16)
        v_aug = jnp.concatenate([v_ref[pl.ds(start, tq), h * HD:(h + 1) * HD], ones], axis=-1)
        return _dot(p, v_aug)

    start, ss = scores(qi, True)
    ps = []
    for h in range(nh):
        m0 = row_max(ss[h])
        m_ref[h] = m0
        ps.append([jnp.exp(sk - m0) for sk in ss[h]])
    for h in range(nh):
        acc_ref[h] = pv(ps[h], start, h)

    def step(j, carry):
        start, ss = scores(j, False)
        ps, alphas = [], []
        for h in range(nh):
            m = m_ref[h]
            m_new = jnp.maximum(m, row_max(ss[h]))
            m_ref[h] = m_new
            alphas.append(jnp.exp(m - m_new))
            ps.append([jnp.exp(sk - m_new) for sk in ss[h]])
        for h in range(nh):
            a2 = jnp.concatenate([alphas[h], alphas[h]], axis=-1)
            acc_ref[h] = a2 * acc_ref[h] + pv(ps[h], start, h)
        return carry

    lax.fori_loop(0, qi, step, 0)
    for h in range(nh):
        acc = acc_ref[h]
        o = acc[:, :HD] / acc[:, HD:]
        o = o * lax.rsqrt(jnp.mean(o * o, axis=-1, keepdims=True) + RMS_EPS) * nw
        o_ref[:, h * HD:(h + 1) * HD] = o.astype(o_ref.dtype)


def fox_prompt(qb, kb, vb, col, row, norm_w, bsz, t_len, nh):
    tq = _pick(t_len, (256, 128))
    assert tq % LANES == 0 and t_len % tq == 0
    nq = t_len // tq
    w = nh * HD
    kern = functools.partial(_fox_prompt_kernel, nh=nh, tq=tq)
    return pl.pallas_call(
        kern,
        grid=(bsz, nq),
        in_specs=[pl.BlockSpec((tq, w), lambda b, i: (b * nq + i, 0)),
                  pl.BlockSpec((t_len, w), lambda b, i: (b, 0)),
                  pl.BlockSpec((t_len, w), lambda b, i: (b, 0)),
                  pl.BlockSpec((None, tq, LANES), lambda b, i: (b, i, 0)),
                  pl.BlockSpec((None, 32, t_len), lambda b, i: (b, 0, 0)),
                  pl.BlockSpec((1, HD), lambda b, i: (0, 0))],
        out_specs=pl.BlockSpec((tq, w), lambda b, i: (b * nq + i, 0)),
        out_shape=jax.ShapeDtypeStruct((bsz * t_len, w), BF16),
        scratch_shapes=[pltpu.VMEM((nh, tq, LANES), F32), pltpu.VMEM((nh, tq, LANES), F32),
                        pltpu.VMEM((nh, tq, 2 * HD), F32)],
        compiler_params=_params(("parallel", "parallel")),
        name="fox_prompt",
    )(qb, kb, vb, col, row, norm_w)


def _fox_sample_kernel(q_ref, kn_ref, vn_ref, kc_ref, vc_ref, lfc_ref, col_ref, row_ref, nw_ref,
                       o_ref, *, nh):
    t_len = q_ref.shape[0]
    past = kc_ref.shape[0]
    scale = HD ** -0.5
    nw = nw_ref[...]
    r = lax.broadcasted_iota(jnp.int32, (past, past), 0)
    c = lax.broadcasted_iota(jnp.int32, (past, past), 1)
    upper = (r <= c).astype(BF16)
    cc = jnp.zeros(lfc_ref.shape, F32)
    for part in _split3(lfc_ref[...]):
        cc = cc + _dot(part, upper)
    rr = lax.broadcasted_iota(jnp.int32, (t_len, t_len), 0)
    rc = lax.broadcasted_iota(jnp.int32, (t_len, t_len), 1)
    causal = rr >= rc
    for h in range(nh):
        cols = slice(h * HD, (h + 1) * HD)
        q = q_ref[:, cols]
        cn_col = col_ref[:, h:h + 1]
        cn_row = row_ref[h:h + 1, :]
        cch = cc[h:h + 1, :]
        tot = cch[:, past - 1:past]
        kc = kc_ref[:, cols].astype(BF16)
        vc = vc_ref[:, cols].astype(BF16)
        s_c = _dot_nt(q, kc) * scale + (cn_col + (tot - cch))
        s_n = _dot_nt(q, kn_ref[:, cols]) * scale + (cn_col - cn_row)
        s_n = jnp.where(causal, s_n, -jnp.inf)
        m = jnp.maximum(jnp.max(s_c, axis=-1, keepdims=True), jnp.max(s_n, axis=-1, keepdims=True))
        p_c = jnp.exp(s_c - m)
        p_n = jnp.exp(s_n - m)
        l = jnp.sum(p_c, axis=-1, keepdims=True) + jnp.sum(p_n, axis=-1, keepdims=True)
        o = (_dot(p_c.astype(BF16), vc) + _dot(p_n.astype(BF16), vn_ref[:, cols])) / l
        o = o * lax.rsqrt(jnp.mean(o * o, axis=-1, keepdims=True) + RMS_EPS) * nw
        o_ref[:, cols] = o.astype(o_ref.dtype)


def fox_sample(qb, knb, vnb, k_cache, v_cache, lf_cache_t, col, row, norm_w, bsz, t_len, nh):
    w = nh * HD
    past = k_cache.shape[1]
    kern = functools.partial(_fox_sample_kernel, nh=nh)
    return pl.pallas_call(
        kern,
        grid=(bsz,),
        in_specs=[pl.BlockSpec((t_len, w), lambda b: (b, 0)),
                  pl.BlockSpec((t_len, w), lambda b: (b, 0)),
                  pl.BlockSpec((t_len, w), lambda b: (b, 0)),
                  pl.BlockSpec((None, past, w), lambda b: (b, 0, 0)),
                  pl.BlockSpec((None, past, w), lambda b: (b, 0, 0)),
                  pl.BlockSpec((None, nh, past), lambda b: (b, 0, 0)),
                  pl.BlockSpec((None, t_len, LANES), lambda b: (b, 0, 0)),
                  pl.BlockSpec((None, 32, t_len), lambda b: (b, 0, 0)),
                  pl.BlockSpec((1, HD), lambda b: (0, 0))],
        out_specs=pl.BlockSpec((t_len, w), lambda b: (b, 0)),
        out_shape=jax.ShapeDtypeStruct((bsz * t_len, w), BF16),
        compiler_params=_params(("parallel",)),
        name="fox_sample",
    )(qb, knb, vnb, k_cache, v_cache, lf_cache_t, col, row, norm_w)


def _gdn_prep_kernel(raw_ref, prev_ref, cbuf_ref, cw_ref, col_ref, row_ref,
                     u_ref, w_ref, qg_ref, kd_ref, qk_ref, ext_ref, *, nh, lc):
    t = pl.program_id(1)
    tb = raw_ref.shape[0]
    width = nh * HD
    pad = 8
    @pl.when(t == 0)
    def _():
        ext_ref[0:pad, :] = cbuf_ref[...]

    @pl.when(t > 0)
    def _():
        ext_ref[0:pad, :] = prev_ref[...]

    ext_ref[pad:pad + tb, :] = raw_ref[...]

    r = lax.broadcasted_iota(jnp.int32, (lc, lc), 0)
    c = lax.broadcasted_iota(jnp.int32, (lc, lc), 1)
    tril = r >= c
    strict = r > c
    eye = (r == c).astype(F32)
    n_sq = int(math.log2(lc)) - 1
    qscale = HD ** -0.5
    zpad = jnp.zeros((lc, HD - lc), BF16) if lc < HD else None

    def conv_act(seg, h):
        lo = seg * width + h * HD
        acc = None
        for i in reversed(range(CONV_W)):
            term = ext_ref[pl.ds(pad - (CONV_W - 1) + i, tb), lo:lo + HD] * cw_ref[i:i + 1, lo:lo + HD]
            acc = term if acc is None else acc + term
        return _silu(acc)

    def l2n(x):
        return x * lax.rsqrt(jnp.sum(x * x, axis=-1, keepdims=True) + L2_EPS)

    units = []
    for h in range(nh):
        cols = slice(h * HD, (h + 1) * HD)
        q_all = l2n(conv_act(0, h)) * qscale
        k_all = l2n(conv_act(1, h))
        v_all = conv_act(2, h)
        for ci in range(tb // lc):
            rows = slice(ci * lc, (ci + 1) * lc)
            q = q_all[rows]
            k = k_all[rows]
            g_col = col_ref[rows, nh + h:nh + h + 1]
            b_col = col_ref[rows, 2 * nh + h:2 * nh + h + 1]
            g_row = row_ref[nh + h:nh + h + 1, rows]
            g_last = g_row[:, lc - 1:lc]
            kb = k * b_col
            qg_ref[rows, cols] = (q * jnp.exp(g_col)).astype(BF16)
            kd_ref[rows, cols] = (k * jnp.exp(g_last - g_col)).astype(BF16)
            units.append(dict(
                rows=rows, cols=cols,
                kbq=jnp.concatenate([kb, q], axis=0).astype(BF16),
                kbf=k.astype(BF16),
                decay=jnp.exp(jnp.where(tril, g_col - g_row, -jnp.inf)),
                rhs=jnp.concatenate([v_all[rows] * b_col, kb * jnp.exp(g_col)], axis=-1).astype(BF16)))

    for un in units:
        akq = _dot_nt(un["kbq"], un["kbf"])
        qkb = (akq[lc:] * un["decay"]).astype(BF16)
        if zpad is not None:
            qkb = jnp.concatenate([qkb, zpad], axis=-1)
        qk_ref[un["rows"], un["cols"]] = qkb
        un["p"] = jnp.where(strict, -(akq[:lc] * un["decay"]), 0.0)
        un["t"] = eye + un["p"]
    for un in units:
        pb = un["p"].astype(BF16)
        un["p"] = _dot(pb, pb)
    for _ in range(n_sq - 1):
        for un in units:
            st = _dot(jnp.concatenate([un["p"], un["t"]], axis=0).astype(BF16), un["p"].astype(BF16))
            un["p"] = st[:lc]
            un["t"] = un["t"] + st[lc:]
    for un in units:
        un["t"] = un["t"] + _dot(un["t"].astype(BF16), un["p"].astype(BF16))
    for un in units:
        uw = _dot(un["t"].astype(BF16), un["rhs"])
        u_ref[un["rows"], un["cols"]] = uw[:, :HD]
        w_ref[un["rows"], un["cols"]] = uw[:, HD:].astype(BF16)


def gdn_prep(raw, conv_state8, conv_w8, col, row, bsz, t_len, nh, lc):
    m, c3 = raw.shape
    width = nh * HD
    tb = _pick(t_len, (128, 64, 32, 16))
    nt = t_len // tb
    kern = functools.partial(_gdn_prep_kernel, nh=nh, lc=lc)
    blk = lambda b, t: (b * nt + t, 0)
    return pl.pallas_call(
        kern,
        grid=(bsz, nt),
        in_specs=[pl.BlockSpec((tb, c3), blk),
                  pl.BlockSpec((8, c3), lambda b, t: (jnp.maximum((b * nt + t) * (tb // 8) - 1, 0), 0)),
                  pl.BlockSpec((None, 8, c3), lambda b, t: (b, 0, 0)),
                  pl.BlockSpec((8, c3), lambda b, t: (0, 0)),
                  pl.BlockSpec((None, tb, LANES), lambda b, t: (b, t, 0)),
                  pl.BlockSpec((None, 32, tb), lambda b, t: (b, 0, t))],
        out_specs=[pl.BlockSpec((tb, width), blk) for _ in range(5)],
        out_shape=[jax.ShapeDtypeStruct((m, width), F32)] +
                  [jax.ShapeDtypeStruct((m, width), BF16) for _ in range(4)],
        scratch_shapes=[pltpu.VMEM((tb + 8, c3), F32)],
        compiler_params=_params(("parallel", "parallel")),
        name="gdn_prep",
    )(raw, raw, conv_state8, conv_w8, col, row)


def _gdn_scan_kernel(u_ref, w_ref, qg_ref, kd_ref, qk_ref, z_ref, row_ref, nw_ref, s0_ref,
                     o_ref, s_ref, *, nh, lc):
    t = pl.program_id(1)
    tb = u_ref.shape[0]
    nw = nw_ref[...]

    @pl.when(t == 0)
    def _():
        s_ref[...] = s0_ref[...]

    heads = [slice(h * HD, (h + 1) * HD) for h in range(nh)]
    states = [s_ref[h] for h in range(nh)]
    for ci in range(tb // lc):
        rows = slice(ci * lc, (ci + 1) * lc)
        wqs = [_dot(jnp.concatenate([w_ref[rows, cols], qg_ref[rows, cols]], axis=0), states[h].astype(BF16))
               for h, cols in enumerate(heads)]
        vbs = [(u_ref[rows, cols] - wqs[h][:lc]).astype(BF16) for h, cols in enumerate(heads)]
        o2s = [_dot(qk_ref[rows, h * HD:h * HD + lc], vbs[h]) for h in range(nh)]
        sds = [_dot_tn(kd_ref[rows, cols], vbs[h]) for h, cols in enumerate(heads)]
        for h, cols in enumerate(heads):
            g_last = row_ref[nh + h:nh + h + 1, ci * lc + lc - 1:ci * lc + lc]
            states[h] = states[h] * jnp.exp(g_last) + sds[h]
            o = wqs[h][lc:] + o2s[h]
            o = o * lax.rsqrt(jnp.mean(o * o, axis=-1, keepdims=True) + RMS_EPS) * nw * _silu(z_ref[rows, cols])
            o_ref[rows, cols] = o.astype(o_ref.dtype)
    for h in range(nh):
        s_ref[h] = states[h]


def gdn_scan(u, w, qg, kd, qk, z, row, norm_w, s0, bsz, t_len, nh, lc):
    m, width = u.shape
    tb = _pick(t_len, (256, 128, 64, 32, 16))
    nt = t_len // tb
    kern = functools.partial(_gdn_scan_kernel, nh=nh, lc=lc)
    blk = lambda b, t: (b * nt + t, 0)
    return pl.pallas_call(
        kern,
        grid=(bsz, nt),
        in_specs=[pl.BlockSpec((tb, width), blk) for _ in range(6)] +
                 [pl.BlockSpec((None, 32, tb), lambda b, t: (b, 0, t)),
                  pl.BlockSpec((1, HD), lambda b, t: (0, 0)),
                  pl.BlockSpec((None, nh, HD, HD), lambda b, t: (b, 0, 0, 0))],
        out_specs=[pl.BlockSpec((tb, width), blk),
                   pl.BlockSpec((None, nh, HD, HD), lambda b, t: (b, 0, 0, 0))],
        out_shape=[jax.ShapeDtypeStruct((m, width), BF16),
                   jax.ShapeDtypeStruct((bsz, nh, HD, HD), F32)],
        compiler_params=_params(("parallel", "arbitrary")),
        name="gdn_scan",
    )(u, w, qg, kd, qk, z, row, norm_w, s0)


def _out_ln_kernel(a_ref, b_ref, wa_ref, wb_ref, res_ref, g_ref, beta_ref, o_ref, ob_ref, *, alpha):
    y = _dot(a_ref[...], wa_ref[...]) + _dot(b_ref[...], wb_ref[...])
    y = _ln_rows(alpha * res_ref[...] + y, g_ref[...], beta_ref[...])
    o_ref[...] = y
    ob_ref[...] = y.astype(BF16)


def out_proj_ln(mix_a, mix_b, w_a, w_b, res, g, beta, alpha):
    m, ka = mix_a.shape
    kb = mix_b.shape[1]
    d = w_a.shape[1]
    tm = _pick(m, (512, 256, 128, 64, 32, 16, 8))
    kern = functools.partial(_out_ln_kernel, alpha=alpha)
    row = lambda i: (i, 0)
    fixed = lambda i: (0, 0)
    return pl.pallas_call(
        kern,
        grid=(m // tm,),
        in_specs=[pl.BlockSpec((tm, ka), row), pl.BlockSpec((tm, kb), row),
                  pl.BlockSpec((ka, d), fixed), pl.BlockSpec((kb, d), fixed),
                  pl.BlockSpec((tm, d), row),
                  pl.BlockSpec((1, d), fixed), pl.BlockSpec((1, d), fixed)],
        out_specs=[pl.BlockSpec((tm, d), row), pl.BlockSpec((tm, d), row)],
        out_shape=[jax.ShapeDtypeStruct((m, d), F32), jax.ShapeDtypeStruct((m, d), BF16)],
        compiler_params=_params(("parallel",)),
        name="out_proj_ln",
    )(mix_a, mix_b, w_a, w_b, res, g.reshape(1, d), beta.reshape(1, d))


def _ffn_up_kernel(x_ref, wg_ref, wu_ref, h_ref):
    x = x_ref[...]
    gate = _dot(x, wg_ref[...])
    up = _dot(x, wu_ref[...])
    h_ref[...] = (_silu(gate) * up).astype(h_ref.dtype)


def ffn_up(xb, wg, wu):
    m, d = xb.shape
    f = wg.shape[1]
    tm = _pick(m, (512, 256, 128, 64, 32, 16, 8))
    tf = _pick(f, (1408, 1024, 512, 256, 128))
    return pl.pallas_call(
        _ffn_up_kernel,
        grid=(f // tf, m // tm),
        in_specs=[pl.BlockSpec((tm, d), lambda j, i: (i, 0)),
                  pl.BlockSpec((d, tf), lambda j, i: (0, j)),
                  pl.BlockSpec((d, tf), lambda j, i: (0, j))],
        out_specs=pl.BlockSpec((tm, tf), lambda j, i: (i, j)),
        out_shape=jax.ShapeDtypeStruct((m, f), BF16),
        compiler_params=_params(("parallel", "parallel")),
        name="ffn_up",
    )(xb, wg, wu)


def _ffn_down_kernel(h_ref, w_ref, res_ref, g_ref, beta_ref, o_ref, ob_ref, acc_ref, *, alpha):
    k = pl.program_id(1)

    @pl.when(k == 0)
    def _():
        acc_ref[...] = jnp.zeros_like(acc_ref)

    acc_ref[...] += _dot(h_ref[...], w_ref[...])

    @pl.when(k == pl.num_programs(1) - 1)
    def _():
        y = _ln_rows(alpha * res_ref[...] + acc_ref[...], g_ref[...], beta_ref[...])
        o_ref[...] = y
        ob_ref[...] = y.astype(BF16)


def ffn_down_ln(h, w, res, g, beta, alpha):
    m, f = h.shape
    d = w.shape[1]
    tm = _pick(m, (512, 256, 128, 64, 32, 16, 8))
    tk = _pick(f, (1408, 1024, 512, 256, 128))
    kern = functools.partial(_ffn_down_kernel, alpha=alpha)
    return pl.pallas_call(
        kern,
        grid=(m // tm, f // tk),
        in_specs=[pl.BlockSpec((tm, tk), lambda i, k: (i, k)),
                  pl.BlockSpec((tk, d), lambda i, k: (k, 0)),
                  pl.BlockSpec((tm, d), lambda i, k: (i, 0)),
                  pl.BlockSpec((1, d), lambda i, k: (0, 0)),
                  pl.BlockSpec((1, d), lambda i, k: (0, 0))],
        out_specs=[pl.BlockSpec((tm, d), lambda i, k: (i, 0)),
                   pl.BlockSpec((tm, d), lambda i, k: (i, 0))],
        out_shape=[jax.ShapeDtypeStruct((m, d), F32), jax.ShapeDtypeStruct((m, d), BF16)],
        scratch_shapes=[pltpu.VMEM((tm, d), F32)],
        compiler_params=_params(("parallel", "arbitrary")),
        name="ffn_down_ln",
    )(h, w, res, g.reshape(1, d), beta.reshape(1, d))


def _layer(x, xb, bsz, t_len, lw, fox_cache, conv_state, s0, alpha):
    nh = lw["nh"]
    width = nh * HD
    lc = min(t_len, CHUNK)
    qb, = matmul(xb, lw["w_fq"], (BF16,))
    kf, kb = matmul(xb, lw["w_fk"], (F32, BF16))
    vf, vb = matmul(xb, lw["w_fv"], (F32, BF16))
    raw, = matmul(xb, lw["w_gqkv"], (F32,))
    z, = matmul(xb, lw["w_gz"], (F32,))
    logf, col, row = gates(xb, lw["w_small"], lw["gate_bias"], lw["alog_row"], bsz, t_len, lc, nh)

    if fox_cache is None:
        mix_f = fox_prompt(qb, kb, vb, col, row, lw["fox_norm_w"], bsz, t_len, nh)
    else:
        k_cache, v_cache, lf_cache = fox_cache
        past = k_cache.shape[1]
        mix_f = fox_sample(qb, kb, vb, k_cache.reshape(bsz, past, width), v_cache.reshape(bsz, past, width),
                           jnp.transpose(lf_cache, (0, 2, 1)), col, row, lw["fox_norm_w"], bsz, t_len, nh)

    conv_state8 = jnp.pad(conv_state, ((0, 0), (8 - (CONV_W - 1), 0), (0, 0)))
    u, w, qg, kd, qk = gdn_prep(raw, conv_state8, lw["conv_w8"], col, row, bsz, t_len, nh, lc)
    mix_g, s_fin = gdn_scan(u, w, qg, kd, qk, z, row, lw["gdn_norm_w"], s0, bsz, t_len, nh, lc)

    x1, x1b = out_proj_ln(mix_f, mix_g, lw["w_out_f"], lw["w_out_g"], x, lw["ln_mix_g"], lw["ln_mix_b"], alpha)
    hmid = ffn_up(x1b, lw["w_gate"], lw["w_up"])
    x2, x2b = ffn_down_ln(hmid, lw["w_down"], x1, lw["ln_ffn_g"], lw["ln_ffn_b"], alpha)

    new_buf = raw.reshape(bsz, t_len, -1)[:, t_len - (CONV_W - 1):, :]
    state = (kf.reshape(bsz, t_len, nh, HD), vf.reshape(bsz, t_len, nh, HD), logf, s_fin, new_buf)
    return x2, x2b, state


def kernel(x_prompt, x_sample, cache_fox_k, cache_fox_v, cache_fox_logf, state_gdn, state_gdn_conv, ln_in_g, ln_in_b, w_in, fox_f_bias, fox_norm_w, gdn_conv_w, gdn_a_log, gdn_dt_bias, gdn_norm_w, w_out, ln_mix_g, ln_mix_b, ffn_w_gate, ffn_w_up, ffn_w_down, ln_ffn_g, ln_ffn_b):
    depth = w_in.shape[0]
    bp, tp, d = x_prompt.shape
    bs, ts, _ = x_sample.shape
    nh = fox_f_bias.shape[1]
    assert fox_norm_w.shape[1] == HD and gdn_norm_w.shape[1] == HD and gdn_a_log.shape[1] == nh
    width = nh * HD
    assert w_in.shape[2] == 7 * width + 3 * nh and gdn_conv_w.shape[2] == 3 * width
    alpha = (2 * depth) ** 0.25

    xp, xpb = layer_norm_rows(x_prompt.reshape(bp * tp, d), ln_in_g, ln_in_b)
    xs, xsb = layer_norm_rows(x_sample.reshape(bs * ts, d), ln_in_g, ln_in_b)

    zero_conv = jnp.zeros((bp, CONV_W - 1, 3 * width), F32)
    zero_state = jnp.zeros((bp, nh, HD, HD), F32)
    ps, ss = [], []
    for l in range(depth):
        wl = w_in[l]
        o_ff = 3 * width
        o_gq = o_ff + nh
        o_ga = o_gq + 3 * width
        o_gz = o_ga + 2 * nh
        pad_lanes = LANES - 3 * nh
        small = jnp.concatenate([wl[:, o_ff:o_gq], wl[:, o_ga:o_gz], jnp.zeros((d, pad_lanes), F32)], axis=1)
        gate_bias = jnp.concatenate([fox_f_bias[l], gdn_dt_bias[l], jnp.zeros((LANES - 2 * nh,), F32)]).reshape(1, LANES)
        alog_row = jnp.concatenate([jnp.zeros((nh,), F32), gdn_a_log[l], jnp.zeros((LANES - 2 * nh,), F32)]).reshape(1, LANES)
        lw = dict(
            nh=nh,
            w_fq=wl[:, 0:width].astype(BF16),
            w_fk=wl[:, width:2 * width].astype(BF16),
            w_fv=wl[:, 2 * width:3 * width].astype(BF16),
            w_gqkv=wl[:, o_gq:o_ga].astype(BF16),
            w_gz=wl[:, o_gz:o_gz + width].astype(BF16),
            w_small=small.astype(BF16),
            gate_bias=gate_bias, alog_row=alog_row,
            fox_norm_w=fox_norm_w[l].reshape(1, HD),
            gdn_norm_w=gdn_norm_w[l].reshape(1, HD),
            conv_w8=jnp.pad(gdn_conv_w[l], ((0, 8 - CONV_W), (0, 0))),
            w_out_f=w_out[l, :width].astype(BF16),
            w_out_g=w_out[l, width:].astype(BF16),
            ln_mix_g=ln_mix_g[l], ln_mix_b=ln_mix_b[l],
            w_gate=ffn_w_gate[l].astype(BF16), w_up=ffn_w_up[l].astype(BF16),
            w_down=ffn_w_down[l].astype(BF16),
            ln_ffn_g=ln_ffn_g[l], ln_ffn_b=ln_ffn_b[l],
        )
        xp, xpb, st_p = _layer(xp, xpb, bp, tp, lw, None, zero_conv, zero_state, alpha)
        xs, xsb, st_s = _layer(xs, xsb, bs, ts, lw,
                               (cache_fox_k[l], cache_fox_v[l], cache_fox_logf[l]),
                               state_gdn_conv[l], state_gdn[l], alpha)
        ps.append(st_p)
        ss.append(st_s)

    def stacked(states, i):
        return jnp.stack([st[i] for st in states], axis=0)

    return (xp.reshape(bp, tp, d), xs.reshape(bs, ts, d),
            stacked(ps, 0), stacked(ps, 1), stacked(ps, 2), stacked(ps, 3), stacked(ps, 4),
            stacked(ss, 0), stacked(ss, 1), stacked(ss, 2), stacked(ss, 3), stacked(ss, 4))
```

```python
import functools
import math

import jax
import jax.numpy as jnp
from jax import lax
from jax.experimental import pallas as pl
from jax.experimental.pallas import tpu as pltpu

LN_EPS = 1e-5
RMS_EPS = 1e-6
L2_EPS = 1e-6
CHUNK = 64
CONV_W = 4
HD = 128
LANES = 128
VMEM_LIMIT = 56 * 1024 * 1024

F32 = jnp.float32
BF16 = jnp.bfloat16


def _pick(dim, cands):
    for c in cands:
        if c <= dim and dim % c == 0:
            return c
    return dim


def _params(sem):
    return pltpu.CompilerParams(dimension_semantics=sem, vmem_limit_bytes=VMEM_LIMIT)


def _split3(x):
    hi = x.astype(BF16)
    r1 = x - hi.astype(F32)
    mid = r1.astype(BF16)
    lo = (r1 - mid.astype(F32)).astype(BF16)
    return hi, mid, lo


def _dot(a, b):
    return jnp.dot(a, b, preferred_element_type=F32)


def _dot_nt(a, b):
    return lax.dot_general(a, b, (((1,), (1,)), ((), ())), preferred_element_type=F32)


def _dot_tn(a, b):
    return lax.dot_general(a, b, (((0,), (0,)), ((), ())), preferred_element_type=F32)


def _ln_rows(y, g, b):
    mu = jnp.mean(y, axis=-1, keepdims=True)
    d = y - mu
    var = jnp.mean(d * d, axis=-1, keepdims=True)
    return d * lax.rsqrt(var + LN_EPS) * g + b


def _silu(x):
    return x / (1.0 + jnp.exp(-x))


def _ln_kernel(x_ref, g_ref, b_ref, o_ref, ob_ref):
    y = _ln_rows(x_ref[...], g_ref[...], b_ref[...])
    o_ref[...] = y
    ob_ref[...] = y.astype(BF16)


def layer_norm_rows(x, g, b):
    m, d = x.shape
    tm = _pick(m, (512, 256, 128, 64, 32, 16, 8))
    return pl.pallas_call(
        _ln_kernel,
        grid=(m // tm,),
        in_specs=[pl.BlockSpec((tm, d), lambda i: (i, 0)),
                  pl.BlockSpec((1, d), lambda i: (0, 0)),
                  pl.BlockSpec((1, d), lambda i: (0, 0))],
        out_specs=[pl.BlockSpec((tm, d), lambda i: (i, 0)),
                   pl.BlockSpec((tm, d), lambda i: (i, 0))],
        out_shape=[jax.ShapeDtypeStruct((m, d), F32), jax.ShapeDtypeStruct((m, d), BF16)],
        compiler_params=_params(("parallel",)),
        name="ln_in",
    )(x, g.reshape(1, d), b.reshape(1, d))


def _cast_kernel(w_ref, o_ref):
    o_ref[...] = w_ref[...].astype(o_ref.dtype)


def cast_bf16(w, layer):
    _, k, n = w.shape
    tk = _pick(k, (512, 256, 128, 64, 32, 16))
    return pl.pallas_call(
        _cast_kernel,
        grid=(k // tk,),
        in_specs=[pl.BlockSpec((None, tk, n), lambda i: (layer, i, 0))],
        out_specs=pl.BlockSpec((tk, n), lambda i: (i, 0)),
        out_shape=jax.ShapeDtypeStruct((k, n), BF16),
        compiler_params=_params(("parallel",)),
        name="cast_bf16",
    )(w)


SEG = 1024
CAST_ROWS = 256


def _proj_kernel(*refs, shift, n_out, n_prev, stack):
    it = iter(refs)
    x_ref = next(it)
    wm_ref = next(it)
    wn_ref = next(it) if shift else None
    prev_ref = next(it) if n_prev else None
    o_refs = [next(it) for _ in range(n_out)]
    stack_ref = next(it) if stack else None
    wb_ref = next(it)

    @pl.when(pl.program_id(1) == 0)
    def _():
        for r0 in range(0, wm_ref.shape[0], CAST_ROWS):
            rows = slice(r0, r0 + CAST_ROWS)
            wm = wm_ref[rows, :]
            if shift:
                wm = jnp.concatenate([wm[:, shift:], wn_ref[rows, 0:shift]], axis=1)
            wb_ref[rows, :] = wm.astype(BF16)

    acc = _dot(x_ref[...], wb_ref[...])
    for o in o_refs:
        o[...] = acc.astype(o.dtype)
    if stack:
        for l in range(n_prev):
            stack_ref[l] = prev_ref[l]
        stack_ref[n_prev] = acc


def proj_segments(xb, w, layer, seg0, nseg, shift, out_dtypes, prev=None, stack=False):
    m, k = xb.shape
    assert k % CAST_ROWS == 0 and not (stack and nseg != 1)
    n_prev = 0 if prev is None else prev.shape[0]
    tm = _pick(m, (1024, 512, 256, 128, 64, 32, 16, 8))
    single = pl.Buffered(1) if nseg == 1 else None
    in_specs = [pl.BlockSpec((tm, k), lambda j, i: (i, 0)),
                pl.BlockSpec((None, k, SEG), lambda j, i: (layer, 0, seg0 + j), pipeline_mode=single)]
    args = [xb, w]
    if shift:
        in_specs.append(pl.BlockSpec((None, k, LANES), lambda j, i: (layer, 0, (seg0 + j + 1) * (SEG // LANES)),
                                     pipeline_mode=single))
        args.append(w)
    if n_prev:
        in_specs.append(pl.BlockSpec((n_prev, tm, SEG), lambda j, i: (0, i, 0)))
        args.append(prev)
    out_specs = [pl.BlockSpec((tm, SEG), lambda j, i: (i, j)) for _ in out_dtypes]
    out_shape = [jax.ShapeDtypeStruct((m, nseg * SEG), dt) for dt in out_dtypes]
    if stack:
        out_specs.append(pl.BlockSpec((n_prev + 1, tm, SEG), lambda j, i: (0, i, 0)))
        out_shape.append(jax.ShapeDtypeStruct((n_prev + 1, m, SEG), F32))
    kern = functools.partial(_proj_kernel, shift=shift, n_out=len(out_dtypes), n_prev=n_prev, stack=stack)
    return pl.pallas_call(
        kern,
        grid=(nseg, m // tm),
        in_specs=in_specs,
        out_specs=out_specs,
        out_shape=out_shape,
        scratch_shapes=[pltpu.VMEM((k, SEG), BF16)],
        compiler_params=_params(("parallel", "arbitrary")),
        name="proj",
    )(*args)


def _gates_kernel(x_ref, wf_ref, wab_ref, bias_ref, alog_ref, logf_ref, col_ref, row_ref, carry_ref, *,
                  sb, chunk, nh):
    t = pl.program_id(1)
    tb = x_ref.shape[0]

    @pl.when(t == 0)
    def _():
        carry_ref[...] = jnp.zeros_like(carry_ref)

    lane = lax.broadcasted_iota(jnp.int32, (1, LANES), 1)
    w = jnp.where(lane < nh, wf_ref[...], wab_ref[...]).astype(BF16)
    raw = _dot(x_ref[...], w) + bias_ref[...]
    sp_neg = jnp.maximum(-raw, 0.0) + jnp.log1p(jnp.exp(-jnp.abs(raw)))
    sp_pos = jnp.maximum(raw, 0.0) + jnp.log1p(jnp.exp(-jnp.abs(raw)))
    logf = -sp_neg
    g = -jnp.exp(alog_ref[...]) * sp_pos
    beta = 1.0 / (1.0 + jnp.exp(-raw))
    is_f = lane < nh
    is_g = (lane >= nh) & (lane < 2 * nh)
    is_b = (lane >= 2 * nh) & (lane < 3 * nh)
    logf_ref[...] = logf[:, :nh]

    r = lax.broadcasted_iota(jnp.int32, (sb, sb), 0)
    c = lax.broadcasted_iota(jnp.int32, (sb, sb), 1)
    tri_full = (r >= c).astype(BF16)
    tri_chunk = ((r >= c) & ((r // chunk) == (c // chunk))).astype(BF16)
    er = lax.broadcasted_iota(jnp.int32, (32, LANES), 0)
    ec = lax.broadcasted_iota(jnp.int32, (32, LANES), 1)
    sel = (er == ec).astype(BF16)

    carry = carry_ref[...]
    for s in range(tb // sb):
        rows = slice(s * sb, (s + 1) * sb)
        lf = jnp.where(is_f, logf[rows], 0.0)
        gg = jnp.where(is_g, g[rows], 0.0)
        cs = jnp.zeros((sb, LANES), F32)
        for part in _split3(lf):
            cs = cs + _dot(tri_full, part)
        for part in _split3(gg):
            cs = cs + _dot(tri_chunk, part)
        cs = cs + carry
        carry = jnp.where(is_f, cs[sb - 1:sb, :], 0.0)
        colv = jnp.where(is_b, beta[rows], cs)
        col_ref[rows, :] = colv
        rv = jnp.zeros((32, sb), F32)
        for part in _split3(colv):
            rv = rv + _dot_nt(sel, part)
        row_ref[:, rows] = rv
    carry_ref[...] = carry


def gates(xb, w, layer, bias_row, alog_row, bsz, t_len, chunk, nh):
    m, d = xb.shape
    tb = _pick(t_len, (512, 256, 128, 64, 32, 16))
    sb = min(tb, 128)
    nt = t_len // tb
    width = nh * HD
    f_blk = 3 * width // LANES
    ab_blk = (6 * width + nh) // LANES
    assert (6 * width + nh) % LANES == nh and 3 * nh <= LANES
    kern = functools.partial(_gates_kernel, sb=sb, chunk=chunk, nh=nh)
    return pl.pallas_call(
        kern,
        grid=(bsz, nt),
        in_specs=[pl.BlockSpec((tb, d), lambda b, t: (b * nt + t, 0)),
                  pl.BlockSpec((None, d, LANES), lambda b, t: (layer, 0, f_blk)),
                  pl.BlockSpec((None, d, LANES), lambda b, t: (layer, 0, ab_blk)),
                  pl.BlockSpec((1, LANES), lambda b, t: (0, 0)),
                  pl.BlockSpec((1, LANES), lambda b, t: (0, 0))],
        out_specs=[pl.BlockSpec((None, tb, nh), lambda b, t: (b, t, 0)),
                   pl.BlockSpec((None, tb, LANES), lambda b, t: (b, t, 0)),
                   pl.BlockSpec((None, 32, tb), lambda b, t: (b, 0, t))],
        out_shape=[jax.ShapeDtypeStruct((bsz, t_len, nh), F32),
                   jax.ShapeDtypeStruct((bsz, t_len, LANES), F32),
                   jax.ShapeDtypeStruct((bsz, 32, t_len), F32)],
        scratch_shapes=[pltpu.VMEM((1, LANES), F32)],
        compiler_params=_params(("parallel", "arbitrary")),
        name="gates",
    )(xb, w, w, bias_row, alog_row)


def _fox_prompt_kernel(q_ref, k_ref, v_ref, col_ref, row_ref, nw_ref, o_ref, cq_ref, m_ref, acc_ref, *,
                       nh, tq):
    qi = pl.program_id(1)
    scale = HD ** -0.5
    nw = nw_ref[...]
    nl = tq // LANES
    r = lax.broadcasted_iota(jnp.int32, (tq, LANES), 0)
    c = lax.broadcasted_iota(jnp.int32, (tq, LANES), 1)
    ones = jnp.ones((tq, LANES), BF16)
    for h in range(nh):
        cq_ref[h] = jnp.broadcast_to(col_ref[:, h:h + 1], (tq, LANES))

    def scores(j, masked):
        start = pl.multiple_of(j * tq, tq)
        out = []
        for h in range(nh):
            cols = slice(h * HD, (h + 1) * HD)
            s = _dot_nt(q_ref[:, cols], k_ref[pl.ds(start, tq), cols])
            cq = cq_ref[h]
            pieces = []
            for k in range(nl):
                ck = row_ref[h:h + 1, pl.ds(start + k * LANES, LANES)]
                sk = s[:, k * LANES:(k + 1) * LANES] * scale + (cq - ck)
                if masked:
                    sk = jnp.where(r >= c + k * LANES, sk, -jnp.inf)
                pieces.append(sk)
            out.append(pieces)
        return start, out

    def row_max(pieces):
        mx = pieces[0]
        for sk in pieces[1:]:
            mx = jnp.maximum(mx, sk)
        return jnp.broadcast_to(jnp.max(mx, axis=-1, keepdims=True), (tq, LANES))

    def pv(p_pieces, start, h):
        p = jnp.concatenate(p_pieces, axis=-1).astype(BF16)
        v_aug = jnp.concatenate([v_ref[pl.ds(start, tq), h * HD:(h + 1) * HD], ones], axis=-1)
        return _dot(p, v_aug)

    start, ss = scores(qi, True)
    ps = []
    for h in range(nh):
        m0 = row_max(ss[h])
        m_ref[h] = m0
        ps.append([jnp.exp(sk - m0) for sk in ss[h]])
    for h in range(nh):
        acc_ref[h] = pv(ps[h], start, h)

    def step(j, carry):
        start, ss = scores(j, False)
        ps, alphas = [], []
        for h in range(nh):
            m = m_ref[h]
            m_new = jnp.maximum(m, row_max(ss[h]))
            m_ref[h] = m_new
            alphas.append(jnp.exp(m - m_new))
            ps.append([jnp.exp(sk - m_new) for sk in ss[h]])
        for h in range(nh):
            a2 = jnp.concatenate([alphas[h], alphas[h]], axis=-1)
            acc_ref[h] = a2 * acc_ref[h] + pv(ps[h], start, h)
        return carry

    lax.fori_loop(0, qi, step, 0)
    for h in range(nh):
        acc = acc_ref[h]
        o = acc[:, :HD] / acc[:, HD:]
        o = o * lax.rsqrt(jnp.mean(o * o, axis=-1, keepdims=True) + RMS_EPS) * nw
        o_ref[:, h * HD:(h + 1) * HD] = o.astype(o_ref.dtype)


def fox_prompt(qb, kb, vb, col, row, norm_w, bsz, t_len, nh):
    tq = _pick(t_len, (256, 128))
    assert tq % LANES == 0 and t_len % tq == 0
    nq = t_len // tq
    w = nh * HD
    kern = functools.partial(_fox_prompt_kernel, nh=nh, tq=tq)
    return pl.pallas_call(
        kern,
        grid=(bsz, nq),
        in_specs=[pl.BlockSpec((tq, w), lambda b, i: (b * nq + i, 0)),
                  pl.BlockSpec((t_len, w), lambda b, i: (b, 0)),
                  pl.BlockSpec((t_len, w), lambda b, i: (b, 0)),
                  pl.BlockSpec((None, tq, LANES), lambda b, i: (b, i, 0)),
                  pl.BlockSpec((None, 32, t_len), lambda b, i: (b, 0, 0)),
                  pl.BlockSpec((1, HD), lambda b, i: (0, 0))],
        out_specs=pl.BlockSpec((tq, w), lambda b, i: (b * nq + i, 0)),
        out_shape=jax.ShapeDtypeStruct((bsz * t_len, w), BF16),
        scratch_shapes=[pltpu.VMEM((nh, tq, LANES), F32), pltpu.VMEM((nh, tq, LANES), F32),
                        pltpu.VMEM((nh, tq, 2 * HD), F32)],
        compiler_params=_params(("parallel", "parallel")),
        name="fox_prompt",
    )(qb, kb, vb, col, row, norm_w)


def _fox_sample_kernel(q_ref, kn_ref, vn_ref, kc_ref, vc_ref, lfc_ref, col_ref, row_ref, nw_ref,
                       o_ref, *, nh):
    t_len = q_ref.shape[0]
    past = kc_ref.shape[0]
    scale = HD ** -0.5
    nw = nw_ref[...]
    r = lax.broadcasted_iota(jnp.int32, (past, past), 0)
    c = lax.broadcasted_iota(jnp.int32, (past, past), 1)
    upper = (r <= c).astype(BF16)
    cc = jnp.zeros(lfc_ref.shape, F32)
    for part in _split3(lfc_ref[...]):
        cc = cc + _dot(part, upper)
    rr = lax.broadcasted_iota(jnp.int32, (t_len, t_len), 0)
    rc = lax.broadcasted_iota(jnp.int32, (t_len, t_len), 1)
    causal = rr >= rc
    for h in range(nh):
        cols = slice(h * HD, (h + 1) * HD)
        q = q_ref[:, cols]
        cn_col = col_ref[:, h:h + 1]
        cn_row = row_ref[h:h + 1, :]
        cch = cc[h:h + 1, :]
        tot = cch[:, past - 1:past]
        kc = kc_ref[:, cols].astype(BF16)
        vc = vc_ref[:, cols].astype(BF16)
        s_c = _dot_nt(q, kc) * scale + (cn_col + (tot - cch))
        s_n = _dot_nt(q, kn_ref[:, cols]) * scale + (cn_col - cn_row)
        s_n = jnp.where(causal, s_n, -jnp.inf)
        m = jnp.maximum(jnp.max(s_c, axis=-1, keepdims=True), jnp.max(s_n, axis=-1, keepdims=True))
        p_c = jnp.exp(s_c - m)
        p_n = jnp.exp(s_n - m)
        l = jnp.sum(p_c, axis=-1, keepdims=True) + jnp.sum(p_n, axis=-1, keepdims=True)
        o = (_dot(p_c.astype(BF16), vc) + _dot(p_n.astype(BF16), vn_ref[:, cols])) / l
        o = o * lax.rsqrt(jnp.mean(o * o, axis=-1, keepdims=True) + RMS_EPS) * nw
        o_ref[:, cols] = o.astype(o_ref.dtype)


def fox_sample(qb, knb, vnb, k_cache, v_cache, layer, lf_cache_t, col, row, norm_w, bsz, t_len, nh):
    w = nh * HD
    past = k_cache.shape[2]
    kern = functools.partial(_fox_sample_kernel, nh=nh)
    return pl.pallas_call(
        kern,
        grid=(bsz,),
        in_specs=[pl.BlockSpec((t_len, w), lambda b: (b, 0)),
                  pl.BlockSpec((t_len, w), lambda b: (b, 0)),
                  pl.BlockSpec((t_len, w), lambda b: (b, 0)),
                  pl.BlockSpec((None, None, past, w), lambda b: (layer, b, 0, 0)),
                  pl.BlockSpec((None, None, past, w), lambda b: (layer, b, 0, 0)),
                  pl.BlockSpec((None, nh, past), lambda b: (b, 0, 0)),
                  pl.BlockSpec((None, t_len, LANES), lambda b: (b, 0, 0)),
                  pl.BlockSpec((None, 32, t_len), lambda b: (b, 0, 0)),
                  pl.BlockSpec((1, HD), lambda b: (0, 0))],
        out_specs=pl.BlockSpec((t_len, w), lambda b: (b, 0)),
        out_shape=jax.ShapeDtypeStruct((bsz * t_len, w), BF16),
        compiler_params=_params(("parallel",)),
        name="fox_sample",
    )(qb, knb, vnb, k_cache, v_cache, lf_cache_t, col, row, norm_w)


def _gdn_prep_kernel(raw_ref, prev_ref, cbuf_ref, cw_ref, col_ref, row_ref,
                     u_ref, w_ref, qg_ref, kd_ref, qk_ref, ext_ref, *, nh, lc):
    t = pl.program_id(1)
    tb = raw_ref.shape[0]
    width = nh * HD
    pad = 8
    @pl.when(t == 0)
    def _():
        ext_ref[0:pad, :] = cbuf_ref[...]

    @pl.when(t > 0)
    def _():
        ext_ref[0:pad, :] = prev_ref[...]

    ext_ref[pad:pad + tb, :] = raw_ref[...]

    r = lax.broadcasted_iota(jnp.int32, (lc, lc), 0)
    c = lax.broadcasted_iota(jnp.int32, (lc, lc), 1)
    tril = r >= c
    strict = r > c
    eye = (r == c).astype(F32)
    n_sq = int(math.log2(lc)) - 1
    qscale = HD ** -0.5
    zpad = jnp.zeros((lc, HD - lc), BF16) if lc < HD else None

    def conv_act(seg, h):
        lo = seg * width + h * HD
        acc = None
        for i in reversed(range(CONV_W)):
            term = ext_ref[pl.ds(pad - (CONV_W - 1) + i, tb), lo:lo + HD] * cw_ref[i:i + 1, lo:lo + HD]
            acc = term if acc is None else acc + term
        return _silu(acc)

    def l2n(x):
        return x * lax.rsqrt(jnp.sum(x * x, axis=-1, keepdims=True) + L2_EPS)

    units = []
    for h in range(nh):
        cols = slice(h * HD, (h + 1) * HD)
        q_all = l2n(conv_act(0, h)) * qscale
        k_all = l2n(conv_act(1, h))
        v_all = conv_act(2, h)
        for ci in range(tb // lc):
            rows = slice(ci * lc, (ci + 1) * lc)
            q = q_all[rows]
            k = k_all[rows]
            g_col = col_ref[rows, nh + h:nh + h + 1]
            b_col = col_ref[rows, 2 * nh + h:2 * nh + h + 1]
            g_row = row_ref[nh + h:nh + h + 1, rows]
            g_last = g_row[:, lc - 1:lc]
            kb = k * b_col
            qg_ref[rows, cols] = (q * jnp.exp(g_col)).astype(BF16)
            kd_ref[rows, cols] = (k * jnp.exp(g_last - g_col)).astype(BF16)
            units.append(dict(
                rows=rows, cols=cols,
                kbq=jnp.concatenate([kb, q], axis=0).astype(BF16),
                kbf=k.astype(BF16),
                decay=jnp.exp(jnp.where(tril, g_col - g_row, -jnp.inf)),
                rhs=jnp.concatenate([v_all[rows] * b_col, kb * jnp.exp(g_col)], axis=-1).astype(BF16)))

    for un in units:
        akq = _dot_nt(un["kbq"], un["kbf"])
        qkb = (akq[lc:] * un["decay"]).astype(BF16)
        if zpad is not None:
            qkb = jnp.concatenate([qkb, zpad], axis=-1)
        qk_ref[un["rows"], un["cols"]] = qkb
        un["p"] = jnp.where(strict, -(akq[:lc] * un["decay"]), 0.0)
        un["t"] = eye + un["p"]
    for un in units:
        pb = un["p"].astype(BF16)
        un["p"] = _dot(pb, pb)
    for _ in range(n_sq - 1):
        for un in units:
            st = _dot(jnp.concatenate([un["p"], un["t"]], axis=0).astype(BF16), un["p"].astype(BF16))
            un["p"] = st[:lc]
            un["t"] = un["t"] + st[lc:]
    for un in units:
        un["t"] = un["t"] + _dot(un["t"].astype(BF16), un["p"].astype(BF16))
    for un in units:
        uw = _dot(un["t"].astype(BF16), un["rhs"])
        u_ref[un["rows"], un["cols"]] = uw[:, :HD]
        w_ref[un["rows"], un["cols"]] = uw[:, HD:].astype(BF16)


def gdn_prep(raw, conv_state8, conv_w8, col, row, bsz, t_len, nh, lc):
    m = raw.shape[0]
    width = nh * HD
    c3 = 3 * width
    tb = _pick(t_len, (128, 64, 32, 16))
    nt = t_len // tb
    kern = functools.partial(_gdn_prep_kernel, nh=nh, lc=lc)
    blk = lambda b, t: (b * nt + t, 0)
    return pl.pallas_call(
        kern,
        grid=(bsz, nt),
        in_specs=[pl.BlockSpec((tb, c3), blk),
                  pl.BlockSpec((8, c3), lambda b, t: (jnp.maximum((b * nt + t) * (tb // 8) - 1, 0), 0)),
                  pl.BlockSpec((None, 8, c3), lambda b, t: (b, 0, 0)),
                  pl.BlockSpec((8, c3), lambda b, t: (0, 0)),
                  pl.BlockSpec((None, tb, LANES), lambda b, t: (b, t, 0)),
                  pl.BlockSpec((None, 32, tb), lambda b, t: (b, 0, t))],
        out_specs=[pl.BlockSpec((tb, width), blk) for _ in range(5)],
        out_shape=[jax.ShapeDtypeStruct((m, width), F32)] +
                  [jax.ShapeDtypeStruct((m, width), BF16) for _ in range(4)],
        scratch_shapes=[pltpu.VMEM((tb + 8, c3), F32)],
        compiler_params=_params(("parallel", "parallel")),
        name="gdn_prep",
    )(raw, raw, conv_state8, conv_w8, col, row)


def _gdn_scan_kernel(u_ref, w_ref, qg_ref, kd_ref, qk_ref, z_ref, row_ref, nw_ref, s0_ref,
                     o_ref, s_ref, *, nh, lc):
    t = pl.program_id(1)
    tb = u_ref.shape[0]
    nw = nw_ref[...]

    @pl.when(t == 0)
    def _():
        s_ref[...] = s0_ref[...]

    heads = [slice(h * HD, (h + 1) * HD) for h in range(nh)]
    states = [s_ref[h] for h in range(nh)]
    for ci in range(tb // lc):
        rows = slice(ci * lc, (ci + 1) * lc)
        wqs = [_dot(jnp.concatenate([w_ref[rows, cols], qg_ref[rows, cols]], axis=0), states[h].astype(BF16))
               for h, cols in enumerate(heads)]
        vbs = [(u_ref[rows, cols] - wqs[h][:lc]).astype(BF16) for h, cols in enumerate(heads)]
        o2s = [_dot(qk_ref[rows, h * HD:h * HD + lc], vbs[h]) for h in range(nh)]
        sds = [_dot_tn(kd_ref[rows, cols], vbs[h]) for h, cols in enumerate(heads)]
        for h, cols in enumerate(heads):
            g_last = row_ref[nh + h:nh + h + 1, ci * lc + lc - 1:ci * lc + lc]
            states[h] = states[h] * jnp.exp(g_last) + sds[h]
            o = wqs[h][lc:] + o2s[h]
            o = o * lax.rsqrt(jnp.mean(o * o, axis=-1, keepdims=True) + RMS_EPS) * nw * _silu(z_ref[rows, cols])
            o_ref[rows, cols] = o.astype(o_ref.dtype)
    for h in range(nh):
        s_ref[h] = states[h]


def gdn_scan(u, w, qg, kd, qk, z, row, norm_w, s0, bsz, t_len, nh, lc):
    m, width = u.shape
    tb = _pick(t_len, (256, 128, 64, 32, 16))
    nt = t_len // tb
    kern = functools.partial(_gdn_scan_kernel, nh=nh, lc=lc)
    blk = lambda b, t: (b * nt + t, 0)
    return pl.pallas_call(
        kern,
        grid=(bsz, nt),
        in_specs=[pl.BlockSpec((tb, width), blk) for _ in range(6)] +
                 [pl.BlockSpec((None, 32, tb), lambda b, t: (b, 0, t)),
                  pl.BlockSpec((1, HD), lambda b, t: (0, 0)),
                  pl.BlockSpec((None, nh, HD, HD), lambda b, t: (b, 0, 0, 0))],
        out_specs=[pl.BlockSpec((tb, width), blk),
                   pl.BlockSpec((None, nh, HD, HD), lambda b, t: (b, 0, 0, 0))],
        out_shape=[jax.ShapeDtypeStruct((m, width), BF16),
                   jax.ShapeDtypeStruct((bsz, nh, HD, HD), F32)],
        compiler_params=_params(("parallel", "arbitrary")),
        name="gdn_scan",
    )(u, w, qg, kd, qk, z, row, norm_w, s0)


def _out_ln_kernel(a_ref, b_ref, w_ref, res_ref, g_ref, beta_ref, o_ref, ob_ref, *, alpha):
    ka = a_ref.shape[1]
    y = _dot(a_ref[...], w_ref[0:ka, :]) + _dot(b_ref[...], w_ref[ka:, :])
    y = _ln_rows(alpha * res_ref[...] + y, g_ref[...], beta_ref[...])
    o_ref[...] = y
    ob_ref[...] = y.astype(BF16)


def out_proj_ln(mix_a, mix_b, w, res, g, beta, alpha):
    m, ka = mix_a.shape
    kb = mix_b.shape[1]
    d = w.shape[1]
    tm = _pick(m, (512, 256, 128, 64, 32, 16, 8))
    kern = functools.partial(_out_ln_kernel, alpha=alpha)
    row = lambda i: (i, 0)
    fixed = lambda i: (0, 0)
    return pl.pallas_call(
        kern,
        grid=(m // tm,),
        in_specs=[pl.BlockSpec((tm, ka), row), pl.BlockSpec((tm, kb), row),
                  pl.BlockSpec((ka + kb, d), fixed, pipeline_mode=pl.Buffered(1)),
                  pl.BlockSpec((tm, d), row),
                  pl.BlockSpec((1, d), fixed), pl.BlockSpec((1, d), fixed)],
        out_specs=[pl.BlockSpec((tm, d), row), pl.BlockSpec((tm, d), row)],
        out_shape=[jax.ShapeDtypeStruct((m, d), F32), jax.ShapeDtypeStruct((m, d), BF16)],
        compiler_params=_params(("parallel",)),
        name="out_proj_ln",
    )(mix_a, mix_b, w, res, g.reshape(1, d), beta.reshape(1, d))


def _ffn_up_kernel(x_ref, wg_ref, wu_ref, h_ref, wgb_ref, wub_ref):
    @pl.when(pl.program_id(1) == 0)
    def _():
        for r0 in range(0, wg_ref.shape[0], CAST_ROWS):
            rows = slice(r0, r0 + CAST_ROWS)
            wgb_ref[rows, :] = wg_ref[rows, :].astype(BF16)
            wub_ref[rows, :] = wu_ref[rows, :].astype(BF16)

    x = x_ref[...]
    gate = _dot(x, wgb_ref[...])
    up = _dot(x, wub_ref[...])
    h_ref[...] = (_silu(gate) * up).astype(h_ref.dtype)


def ffn_up(xb, wg, wu, layer):
    m, d = xb.shape
    f = wg.shape[2]
    assert d % CAST_ROWS == 0
    tm = _pick(m, (1024, 512, 256, 128, 64, 32, 16, 8))
    tf = _pick(f, (512, 256, 128))
    wspec = pl.BlockSpec((None, d, tf), lambda j, i: (layer, 0, j))
    return pl.pallas_call(
        _ffn_up_kernel,
        grid=(f // tf, m // tm),
        in_specs=[pl.BlockSpec((tm, d), lambda j, i: (i, 0)), wspec, wspec],
        out_specs=pl.BlockSpec((tm, tf), lambda j, i: (i, j)),
        out_shape=jax.ShapeDtypeStruct((m, f), BF16),
        scratch_shapes=[pltpu.VMEM((d, tf), BF16), pltpu.VMEM((d, tf), BF16)],
        compiler_params=_params(("parallel", "arbitrary")),
        name="ffn_up",
    )(xb, wg, wu)


def _ffn_down_kernel(h_ref, w_ref, res_ref, g_ref, beta_ref, o_ref, ob_ref, *, alpha):
    y = _dot(h_ref[...], w_ref[...])
    y = _ln_rows(alpha * res_ref[...] + y, g_ref[...], beta_ref[...])
    o_ref[...] = y
    ob_ref[...] = y.astype(BF16)


def ffn_down_ln(h, w, res, g, beta, alpha):
    m, f = h.shape
    d = w.shape[1]
    tm = _pick(m, (256, 128, 64, 32, 16, 8))
    kern = functools.partial(_ffn_down_kernel, alpha=alpha)
    row = lambda i: (i, 0)
    fixed = lambda i: (0, 0)
    return pl.pallas_call(
        kern,
        grid=(m // tm,),
        in_specs=[pl.BlockSpec((tm, f), row),
                  pl.BlockSpec((f, d), fixed, pipeline_mode=pl.Buffered(1)),
                  pl.BlockSpec((tm, d), row),
                  pl.BlockSpec((1, d), fixed), pl.BlockSpec((1, d), fixed)],
        out_specs=[pl.BlockSpec((tm, d), row), pl.BlockSpec((tm, d), row)],
        out_shape=[jax.ShapeDtypeStruct((m, d), F32), jax.ShapeDtypeStruct((m, d), BF16)],
        compiler_params=_params(("parallel",)),
        name="ffn_down_ln",
    )(h, w, res, g.reshape(1, d), beta.reshape(1, d))


def _layer(x, xb, bsz, t_len, layer, wts, lw, fox_cache, conv_state, s0, kv_prev, alpha):
    nh = lw["nh"]
    width = nh * HD
    lc = min(t_len, CHUNK)
    w_in = wts["w_in"]
    k_prev, v_prev = kv_prev if kv_prev is not None else (None, None)
    qb, = proj_segments(xb, w_in, layer, 0, 1, 0, (BF16,))
    kb, k_stack = proj_segments(xb, w_in, layer, 1, 1, 0, (BF16,), prev=k_prev, stack=True)
    vb, v_stack = proj_segments(xb, w_in, layer, 2, 1, 0, (BF16,), prev=v_prev, stack=True)
    raw, = proj_segments(xb, w_in, layer, 3, 3, nh, (F32,))
    z, = proj_segments(xb, w_in, layer, 6, 1, 3 * nh, (F32,))
    logf, col, row = gates(xb, w_in, layer, lw["gate_bias"], lw["alog_row"], bsz, t_len, lc, nh)

    if fox_cache is None:
        mix_f = fox_prompt(qb, kb, vb, col, row, lw["fox_norm_w"], bsz, t_len, nh)
    else:
        k_cache, v_cache, lf_cache = fox_cache
        mix_f = fox_sample(qb, kb, vb, k_cache, v_cache, layer, jnp.transpose(lf_cache, (0, 2, 1)), col, row,
                           lw["fox_norm_w"], bsz, t_len, nh)

    conv_state8 = jnp.pad(conv_state, ((0, 0), (8 - (CONV_W - 1), 0), (0, 0)))
    u, w, qg, kd, qk = gdn_prep(raw, conv_state8, lw["conv_w8"], col, row, bsz, t_len, nh, lc)
    mix_g, s_fin = gdn_scan(u, w, qg, kd, qk, z, row, lw["gdn_norm_w"], s0, bsz, t_len, nh, lc)

    x1, x1b = out_proj_ln(mix_f, mix_g, lw["w_out"], x, lw["ln_mix_g"], lw["ln_mix_b"], alpha)
    hmid = ffn_up(x1b, wts["w_gate"], wts["w_up"], layer)
    x2, x2b = ffn_down_ln(hmid, lw["w_down"], x1, lw["ln_ffn_g"], lw["ln_ffn_b"], alpha)

    new_buf = raw.reshape(bsz, t_len, -1)[:, t_len - (CONV_W - 1):, :]
    return x2, x2b, (k_stack, v_stack), (logf, s_fin, new_buf)


def kernel(x_prompt, x_sample, cache_fox_k, cache_fox_v, cache_fox_logf, state_gdn, state_gdn_conv, ln_in_g, ln_in_b, w_in, fox_f_bias, fox_norm_w, gdn_conv_w, gdn_a_log, gdn_dt_bias, gdn_norm_w, w_out, ln_mix_g, ln_mix_b, ffn_w_gate, ffn_w_up, ffn_w_down, ln_ffn_g, ln_ffn_b):
    depth = w_in.shape[0]
    bp, tp, d = x_prompt.shape
    bs, ts, _ = x_sample.shape
    nh = fox_f_bias.shape[1]
    assert fox_norm_w.shape[1] == HD and gdn_norm_w.shape[1] == HD and gdn_a_log.shape[1] == nh
    width = nh * HD
    assert width == SEG and w_in.shape[2] == 7 * width + 3 * nh and gdn_conv_w.shape[2] == 3 * width
    alpha = (2 * depth) ** 0.25

    xp, xpb = layer_norm_rows(x_prompt.reshape(bp * tp, d), ln_in_g, ln_in_b)
    xs, xsb = layer_norm_rows(x_sample.reshape(bs * ts, d), ln_in_g, ln_in_b)

    zero_conv = jnp.zeros((bp, CONV_W - 1, 3 * width), F32)
    zero_state = jnp.zeros((bp, nh, HD, HD), F32)
    past = cache_fox_k.shape[2]
    k_cache = cache_fox_k.reshape(depth, bs, past, width)
    v_cache = cache_fox_v.reshape(depth, bs, past, width)
    wts = dict(w_in=w_in, w_gate=ffn_w_gate, w_up=ffn_w_up)
    kv_p = kv_s = None
    ps, ss = [], []
    for l in range(depth):
        gate_bias = jnp.concatenate([fox_f_bias[l], gdn_dt_bias[l], jnp.zeros((LANES - 2 * nh,), F32)]).reshape(1, LANES)
        alog_row = jnp.concatenate([jnp.zeros((nh,), F32), gdn_a_log[l], jnp.zeros((LANES - 2 * nh,), F32)]).reshape(1, LANES)
        lw = dict(
            nh=nh,
            gate_bias=gate_bias, alog_row=alog_row,
            fox_norm_w=fox_norm_w[l].reshape(1, HD),
            gdn_norm_w=gdn_norm_w[l].reshape(1, HD),
            conv_w8=jnp.pad(gdn_conv_w[l], ((0, 8 - CONV_W), (0, 0))),
            w_out=cast_bf16(w_out, l),
            ln_mix_g=ln_mix_g[l], ln_mix_b=ln_mix_b[l],
            w_down=cast_bf16(ffn_w_down, l),
            ln_ffn_g=ln_ffn_g[l], ln_ffn_b=ln_ffn_b[l],
        )
        xp, xpb, kv_p, st_p = _layer(xp, xpb, bp, tp, l, wts, lw, None, zero_conv, zero_state, kv_p, alpha)
        xs, xsb, kv_s, st_s = _layer(xs, xsb, bs, ts, l, wts, lw, (k_cache, v_cache, cache_fox_logf[l]),
                                     state_gdn_conv[l], state_gdn[l], kv_s, alpha)
        ps.append(st_p)
        ss.append(st_s)

    def stacked(states, i):
        return jnp.stack([st[i] for st in states], axis=0)

    def heads(stack, bsz, t_len):
        return stack.reshape(depth, bsz, t_len, nh, HD)

    return (xp.reshape(bp, tp, d), xs.reshape(bs, ts, d),
            heads(kv_p[0], bp, tp), heads(kv_p[1], bp, tp), stacked(ps, 0), stacked(ps, 1), stacked(ps, 2),
            heads(kv_s[0], bs, ts), heads(kv_s[1], bs, ts), stacked(ss, 0), stacked(ss, 1), stacked(ss, 2))
```

```python
import functools
import math

import jax
import jax.numpy as jnp
from jax import lax
from jax.experimental import pallas as pl
from jax.experimental.pallas import tpu as pltpu

LN_EPS = 1e-5
RMS_EPS = 1e-6
L2_EPS = 1e-6
CHUNK = 64
CONV_W = 4
HD = 128
LANES = 128
VMEM_LIMIT = 56 * 1024 * 1024

F32 = jnp.float32
BF16 = jnp.bfloat16


def _pick(dim, cands):
    for c in cands:
        if c <= dim and dim % c == 0:
            return c
    return dim


def _params(sem):
    return pltpu.CompilerParams(dimension_semantics=sem, vmem_limit_bytes=VMEM_LIMIT)


def _split3(x):
    hi = x.astype(BF16)
    r1 = x - hi.astype(F32)
    mid = r1.astype(BF16)
    lo = (r1 - mid.astype(F32)).astype(BF16)
    return hi, mid, lo


def _dot(a, b):
    return jnp.dot(a, b, preferred_element_type=F32)


def _dot_nt(a, b):
    return lax.dot_general(a, b, (((1,), (1,)), ((), ())), preferred_element_type=F32)


def _dot_tn(a, b):
    return lax.dot_general(a, b, (((0,), (0,)), ((), ())), preferred_element_type=F32)


def _ln_rows(y, g, b):
    mu = jnp.mean(y, axis=-1, keepdims=True)
    d = y - mu
    var = jnp.mean(d * d, axis=-1, keepdims=True)
    return d * lax.rsqrt(var + LN_EPS) * g + b


def _silu(x):
    return x / (1.0 + jnp.exp(-x))


def _ln_kernel(x_ref, g_ref, b_ref, o_ref, ob_ref):
    y = _ln_rows(x_ref[...], g_ref[...], b_ref[...])
    o_ref[...] = y
    ob_ref[...] = y.astype(BF16)


def layer_norm_rows(x, g, b):
    m, d = x.shape
    tm = _pick(m, (512, 256, 128, 64, 32, 16, 8))
    return pl.pallas_call(
        _ln_kernel,
        grid=(m // tm,),
        in_specs=[pl.BlockSpec((tm, d), lambda i: (i, 0)),
                  pl.BlockSpec((1, d), lambda i: (0, 0)),
                  pl.BlockSpec((1, d), lambda i: (0, 0))],
        out_specs=[pl.BlockSpec((tm, d), lambda i: (i, 0)),
                   pl.BlockSpec((tm, d), lambda i: (i, 0))],
        out_shape=[jax.ShapeDtypeStruct((m, d), F32), jax.ShapeDtypeStruct((m, d), BF16)],
        compiler_params=_params(("parallel",)),
        name="ln_in",
    )(x, g.reshape(1, d), b.reshape(1, d))


def _cast_kernel(w_ref, o_ref):
    o_ref[...] = w_ref[...].astype(o_ref.dtype)


def cast_bf16(w, layer):
    _, k, n = w.shape
    tk = _pick(k, (512, 256, 128, 64, 32, 16))
    return pl.pallas_call(
        _cast_kernel,
        grid=(k // tk,),
        in_specs=[pl.BlockSpec((None, tk, n), lambda i: (layer, i, 0))],
        out_specs=pl.BlockSpec((tk, n), lambda i: (i, 0)),
        out_shape=jax.ShapeDtypeStruct((k, n), BF16),
        compiler_params=_params(("parallel",)),
        name="cast_bf16",
    )(w)


HEAD_GROUP = 4
SEG = 1024
CAST_ROWS = 256


def _proj_kernel(*refs, shift, n_out, n_prev, stack, out_scale):
    it = iter(refs)
    x_ref = next(it)
    wm_ref = next(it)
    wn_ref = next(it) if shift else None
    prev_ref = next(it) if n_prev else None
    o_refs = [next(it) for _ in range(n_out)]
    stack_ref = next(it) if stack else None
    wb_ref = next(it)

    @pl.when(pl.program_id(1) == 0)
    def _():
        for r0 in range(0, wm_ref.shape[0], CAST_ROWS):
            rows = slice(r0, r0 + CAST_ROWS)
            wm = wm_ref[rows, :]
            if shift:
                wm = jnp.concatenate([wm[:, shift:], wn_ref[rows, 0:shift]], axis=1)
            wb_ref[rows, :] = wm.astype(BF16)

    acc = _dot(x_ref[...], wb_ref[...])
    for o in o_refs:
        o[...] = (acc if out_scale is None else acc * out_scale).astype(o.dtype)
    if stack:
        for l in range(n_prev):
            stack_ref[l] = prev_ref[l]
        for h in range(SEG // HD):
            stack_ref[n_prev, :, h, :] = acc[:, h * HD:(h + 1) * HD]


def proj_segments(xb, w, layer, seg0, nseg, shift, out_dtypes, prev=None, stack=False, out_scale=None):
    m, k = xb.shape
    assert k % CAST_ROWS == 0 and not (stack and nseg != 1)
    n_prev = 0 if prev is None else prev.shape[0]
    tm = _pick(m, (1024, 512, 256, 128, 64, 32, 16, 8))
    single = pl.Buffered(1) if nseg == 1 else None
    in_specs = [pl.BlockSpec((tm, k), lambda j, i: (i, 0)),
                pl.BlockSpec((None, k, SEG), lambda j, i: (layer, 0, seg0 + j), pipeline_mode=single)]
    args = [xb, w]
    if shift:
        in_specs.append(pl.BlockSpec((None, k, LANES), lambda j, i: (layer, 0, (seg0 + j + 1) * (SEG // LANES)),
                                     pipeline_mode=single))
        args.append(w)
    if n_prev:
        in_specs.append(pl.BlockSpec((n_prev, tm, SEG // HD, HD), lambda j, i: (0, i, 0, 0)))
        args.append(prev)
    out_specs = [pl.BlockSpec((tm, SEG), lambda j, i: (i, j)) for _ in out_dtypes]
    out_shape = [jax.ShapeDtypeStruct((m, nseg * SEG), dt) for dt in out_dtypes]
    if stack:
        out_specs.append(pl.BlockSpec((n_prev + 1, tm, SEG // HD, HD), lambda j, i: (0, i, 0, 0)))
        out_shape.append(jax.ShapeDtypeStruct((n_prev + 1, m, SEG // HD, HD), F32))
    kern = functools.partial(_proj_kernel, shift=shift, n_out=len(out_dtypes), n_prev=n_prev, stack=stack,
                             out_scale=out_scale)
    return pl.pallas_call(
        kern,
        grid=(nseg, m // tm),
        in_specs=in_specs,
        out_specs=out_specs,
        out_shape=out_shape,
        scratch_shapes=[pltpu.VMEM((k, SEG), BF16)],
        compiler_params=_params(("parallel", "arbitrary")),
        name="proj",
    )(*args)


def _gates_kernel(x_ref, wf_ref, wab_ref, bias_ref, alog_ref, logf_ref, col_ref, row_ref, qa_ref, ka_ref,
                  carry_ref, *, sb, chunk, nh):
    t = pl.program_id(1)
    tb = x_ref.shape[0]

    @pl.when(t == 0)
    def _():
        carry_ref[...] = jnp.zeros_like(carry_ref)

    lane = lax.broadcasted_iota(jnp.int32, (1, LANES), 1)
    w = jnp.where(lane < nh, wf_ref[...], wab_ref[...]).astype(BF16)
    raw = _dot(x_ref[...], w) + bias_ref[...]
    sp_neg = jnp.maximum(-raw, 0.0) + jnp.log1p(jnp.exp(-jnp.abs(raw)))
    sp_pos = jnp.maximum(raw, 0.0) + jnp.log1p(jnp.exp(-jnp.abs(raw)))
    logf = -sp_neg
    g = -jnp.exp(alog_ref[...]) * sp_pos
    beta = 1.0 / (1.0 + jnp.exp(-raw))
    is_f = lane < nh
    is_g = (lane >= nh) & (lane < 2 * nh)
    is_b = (lane >= 2 * nh) & (lane < 3 * nh)
    logf_ref[...] = logf[:, :nh]

    r = lax.broadcasted_iota(jnp.int32, (sb, sb), 0)
    c = lax.broadcasted_iota(jnp.int32, (sb, sb), 1)
    tri_full = (r >= c).astype(BF16)
    tri_chunk = ((r >= c) & ((r // chunk) == (c // chunk))).astype(BF16)
    er = lax.broadcasted_iota(jnp.int32, (32, LANES), 0)
    ec = lax.broadcasted_iota(jnp.int32, (32, LANES), 1)
    sel = (er == ec).astype(BF16)
    wide = nh * HD
    gr = lax.broadcasted_iota(jnp.int32, (LANES, wide), 0)
    gc = lax.broadcasted_iota(jnp.int32, (LANES, wide), 1)
    spread_q = [((gc == gr * HD + (3 + j)) & (gr < nh)).astype(BF16) for j in range(3)]
    spread_k = [((gc == gr * HD + j) & (gr < nh)).astype(BF16) for j in range(3)]
    lane_in_head = lax.broadcasted_iota(jnp.int32, (1, wide), 1) % HD
    ones_q = (lane_in_head < 3).astype(F32)
    ones_k = ((lane_in_head >= 3) & (lane_in_head < 6)).astype(F32)

    carry = carry_ref[...]
    for s in range(tb // sb):
        rows = slice(s * sb, (s + 1) * sb)
        lf = jnp.where(is_f, logf[rows], 0.0)
        gg = jnp.where(is_g, g[rows], 0.0)
        cs = jnp.zeros((sb, LANES), F32)
        for part in _split3(lf):
            cs = cs + _dot(tri_full, part)
        for part in _split3(gg):
            cs = cs + _dot(tri_chunk, part)
        cs = cs + carry
        carry = jnp.where(is_f, cs[sb - 1:sb, :], 0.0)
        colv = jnp.where(is_b, beta[rows], cs)
        col_ref[rows, :] = colv
        rv = jnp.zeros((32, sb), F32)
        qa = ones_q
        ka = ones_k
        for j, part in enumerate(_split3(colv)):
            rv = rv + _dot_nt(sel, part)
            qa = qa + _dot(part, spread_q[j])
            ka = ka - _dot(part, spread_k[j])
        row_ref[:, rows] = rv
        qa_ref[rows, :] = qa.astype(BF16)
        ka_ref[rows, :] = ka.astype(BF16)
    carry_ref[...] = carry


def gates(xb, w, layer, bias_row, alog_row, bsz, t_len, chunk, nh):
    m, d = xb.shape
    tb = _pick(t_len, (512, 256, 128, 64, 32, 16))
    sb = min(tb, 128)
    nt = t_len // tb
    width = nh * HD
    f_blk = 3 * width // LANES
    ab_blk = (6 * width + nh) // LANES
    assert (6 * width + nh) % LANES == nh and 3 * nh <= LANES
    kern = functools.partial(_gates_kernel, sb=sb, chunk=chunk, nh=nh)
    return pl.pallas_call(
        kern,
        grid=(bsz, nt),
        in_specs=[pl.BlockSpec((tb, d), lambda b, t: (b * nt + t, 0)),
                  pl.BlockSpec((None, d, LANES), lambda b, t: (layer, 0, f_blk)),
                  pl.BlockSpec((None, d, LANES), lambda b, t: (layer, 0, ab_blk)),
                  pl.BlockSpec((1, LANES), lambda b, t: (0, 0)),
                  pl.BlockSpec((1, LANES), lambda b, t: (0, 0))],
        out_specs=[pl.BlockSpec((None, tb, nh), lambda b, t: (b, t, 0)),
                   pl.BlockSpec((None, tb, LANES), lambda b, t: (b, t, 0)),
                   pl.BlockSpec((None, 32, tb), lambda b, t: (b, 0, t)),
                   pl.BlockSpec((tb, width), lambda b, t: (b * nt + t, 0)),
                   pl.BlockSpec((tb, width), lambda b, t: (b * nt + t, 0))],
        out_shape=[jax.ShapeDtypeStruct((bsz, t_len, nh), F32),
                   jax.ShapeDtypeStruct((bsz, t_len, LANES), F32),
                   jax.ShapeDtypeStruct((bsz, 32, t_len), F32),
                   jax.ShapeDtypeStruct((m, width), BF16),
                   jax.ShapeDtypeStruct((m, width), BF16)],
        scratch_shapes=[pltpu.VMEM((1, LANES), F32)],
        compiler_params=_params(("parallel", "arbitrary")),
        name="gates",
    )(xb, w, w, bias_row, alog_row)


def _fox_prompt_kernel(q_ref, qa_ref, k_ref, ka_ref, v_ref, nw_ref, o_ref, q2_ref, m_ref, acc_ref, *, nh, tq):
    qi = pl.program_id(1)
    nw = nw_ref[...]
    r = lax.broadcasted_iota(jnp.int32, (tq, LANES), 0)
    c = lax.broadcasted_iota(jnp.int32, (tq, LANES), 1)
    heads = [slice(h * HD, (h + 1) * HD) for h in range(nh)]
    for h, cols in enumerate(heads):
        q2_ref[h] = jnp.concatenate([q_ref[:, cols], qa_ref[:, cols]], axis=-1)

    def scores(start, width, group, masked):
        out = []
        for h in group:
            cols = heads[h]
            k2 = jnp.concatenate([k_ref[pl.ds(start, width), cols], ka_ref[pl.ds(start, width), cols]], axis=-1)
            s = _dot_nt(q2_ref[h], k2)
            pieces = [s[:, k * LANES:(k + 1) * LANES] for k in range(width // LANES)]
            if masked:
                pieces = [jnp.where(r >= c + k * LANES, sk, -jnp.inf) for k, sk in enumerate(pieces)]
            out.append(pieces)
        return out

    def row_max(pieces):
        mx = pieces[0]
        for sk in pieces[1:]:
            mx = jnp.maximum(mx, sk)
        return jnp.broadcast_to(jnp.max(mx, axis=-1, keepdims=True), (tq, LANES))

    def pv(p_pieces, start, width, h):
        p = jnp.concatenate(p_pieces, axis=-1).astype(BF16)
        v_aug = jnp.concatenate([v_ref[pl.ds(start, width), heads[h]], jnp.ones((width, LANES), BF16)], axis=-1)
        return _dot(p, v_aug)

    groups = [list(range(g, min(g + HEAD_GROUP, nh))) for g in range(0, nh, HEAD_GROUP)]

    def block(start, width, first):
        pending = scores(start, width, groups[0], first)
        for gi, group in enumerate(groups):
            ss = pending
            if gi + 1 < len(groups):
                pending = scores(start, width, groups[gi + 1], first)
            ps, alphas = [], []
            for h, pieces in zip(group, ss):
                mx = row_max(pieces)
                if first:
                    m_new = mx
                else:
                    m = m_ref[h]
                    m_new = jnp.maximum(m, mx)
                    alphas.append(jnp.exp(m - m_new))
                m_ref[h] = m_new
                ps.append([jnp.exp(sk - m_new) for sk in pieces])
            for i, (h, pp) in enumerate(zip(group, ps)):
                if first:
                    acc_ref[h] = pv(pp, start, width, h)
                else:
                    a = alphas[i]
                    acc_ref[h] = jnp.concatenate([a, a], axis=-1) * acc_ref[h] + pv(pp, start, width, h)

    block(pl.multiple_of(qi * tq, tq), tq, True)

    @pl.when(qi % 2 == 1)
    def _():
        block(pl.multiple_of((qi - 1) * tq, tq), tq, False)

    def step(j, carry):
        block(pl.multiple_of(j * (2 * tq), 2 * tq), 2 * tq, False)
        return carry

    lax.fori_loop(0, qi // 2, step, 0)
    for h in range(nh):
        acc = acc_ref[h]
        o = acc[:, :HD] / acc[:, HD:]
        o = o * lax.rsqrt(jnp.mean(o * o, axis=-1, keepdims=True) + RMS_EPS) * nw
        o_ref[:, heads[h]] = o.astype(o_ref.dtype)


def fox_prompt(qb, qa, kb, ka, vb, norm_w, bsz, t_len, nh):
    tq = _pick(t_len, (256, 128))
    assert tq % LANES == 0 and t_len % tq == 0
    nq = t_len // tq
    w = nh * HD
    kern = functools.partial(_fox_prompt_kernel, nh=nh, tq=tq)
    qspec = pl.BlockSpec((tq, w), lambda b, i: (b * nq + i, 0))
    kspec = pl.BlockSpec((t_len, w), lambda b, i: (b, 0))
    return pl.pallas_call(
        kern,
        grid=(bsz, nq),
        in_specs=[qspec, qspec, kspec, kspec, kspec, pl.BlockSpec((1, HD), lambda b, i: (0, 0))],
        out_specs=pl.BlockSpec((tq, w), lambda b, i: (b * nq + i, 0)),
        out_shape=jax.ShapeDtypeStruct((bsz * t_len, w), BF16),
        scratch_shapes=[pltpu.VMEM((nh, tq, 2 * HD), BF16), pltpu.VMEM((nh, tq, LANES), F32),
                        pltpu.VMEM((nh, tq, 2 * HD), F32)],
        compiler_params=_params(("parallel", "parallel")),
        name="fox_prompt",
    )(qb, qa, kb, ka, vb, norm_w)


def _fox_sample_kernel(q_ref, kn_ref, vn_ref, kc_ref, vc_ref, lfc_ref, col_ref, row_ref, nw_ref,
                       o_ref, *, nh):
    t_len = q_ref.shape[0]
    past = kc_ref.shape[0]
    nw = nw_ref[...]
    r = lax.broadcasted_iota(jnp.int32, (past, past), 0)
    c = lax.broadcasted_iota(jnp.int32, (past, past), 1)
    upper = (r <= c).astype(BF16)
    cc = jnp.zeros(lfc_ref.shape, F32)
    for part in _split3(lfc_ref[...]):
        cc = cc + _dot(part, upper)
    rr = lax.broadcasted_iota(jnp.int32, (t_len, t_len), 0)
    rc = lax.broadcasted_iota(jnp.int32, (t_len, t_len), 1)
    causal = rr >= rc
    for h in range(nh):
        cols = slice(h * HD, (h + 1) * HD)
        q = q_ref[:, cols]
        cn_col = col_ref[:, h:h + 1]
        cn_row = row_ref[h:h + 1, :]
        cch = cc[h:h + 1, :]
        tot = cch[:, past - 1:past]
        kc = kc_ref[:, h, :].astype(BF16)
        vc = vc_ref[:, h, :].astype(BF16)
        s_c = _dot_nt(q, kc) + (cn_col + (tot - cch))
        s_n = _dot_nt(q, kn_ref[:, cols]) + (cn_col - cn_row)
        s_n = jnp.where(causal, s_n, -jnp.inf)
        m = jnp.maximum(jnp.max(s_c, axis=-1, keepdims=True), jnp.max(s_n, axis=-1, keepdims=True))
        p_c = jnp.exp(s_c - m)
        p_n = jnp.exp(s_n - m)
        l = jnp.sum(p_c, axis=-1, keepdims=True) + jnp.sum(p_n, axis=-1, keepdims=True)
        o = (_dot(p_c.astype(BF16), vc) + _dot(p_n.astype(BF16), vn_ref[:, cols])) / l
        o = o * lax.rsqrt(jnp.mean(o * o, axis=-1, keepdims=True) + RMS_EPS) * nw
        o_ref[:, cols] = o.astype(o_ref.dtype)


def fox_sample(qb, knb, vnb, k_cache, v_cache, layer, lf_cache_t, col, row, norm_w, bsz, t_len, nh):
    w = nh * HD
    past = k_cache.shape[2]
    kern = functools.partial(_fox_sample_kernel, nh=nh)
    return pl.pallas_call(
        kern,
        grid=(bsz,),
        in_specs=[pl.BlockSpec((t_len, w), lambda b: (b, 0)),
                  pl.BlockSpec((t_len, w), lambda b: (b, 0)),
                  pl.BlockSpec((t_len, w), lambda b: (b, 0)),
                  pl.BlockSpec((None, None, past, nh, HD), lambda b: (layer, b, 0, 0, 0)),
                  pl.BlockSpec((None, None, past, nh, HD), lambda b: (layer, b, 0, 0, 0)),
                  pl.BlockSpec((None, nh, past), lambda b: (b, 0, 0)),
                  pl.BlockSpec((None, t_len, LANES), lambda b: (b, 0, 0)),
                  pl.BlockSpec((None, 32, t_len), lambda b: (b, 0, 0)),
                  pl.BlockSpec((1, HD), lambda b: (0, 0))],
        out_specs=pl.BlockSpec((t_len, w), lambda b: (b, 0)),
        out_shape=jax.ShapeDtypeStruct((bsz * t_len, w), BF16),
        compiler_params=_params(("parallel",)),
        name="fox_sample",
    )(qb, knb, vnb, k_cache, v_cache, lf_cache_t, col, row, norm_w)


def _gdn_prep_kernel(raw_ref, prev_ref, cbuf_ref, cw_ref, col_ref, row_ref,
                     u_ref, w_ref, qg_ref, kd_ref, qk_ref, ext_ref, *, nh, lc):
    t = pl.program_id(1)
    tb = raw_ref.shape[0]
    width = nh * HD
    pad = 8
    @pl.when(t == 0)
    def _():
        ext_ref[0:pad, :] = cbuf_ref[...]

    @pl.when(t > 0)
    def _():
        ext_ref[0:pad, :] = prev_ref[...]

    ext_ref[pad:pad + tb, :] = raw_ref[...]

    r = lax.broadcasted_iota(jnp.int32, (lc, lc), 0)
    c = lax.broadcasted_iota(jnp.int32, (lc, lc), 1)
    tril = r >= c
    strict = r > c
    eye = (r == c).astype(F32)
    n_sq = int(math.log2(lc)) - 1
    qscale = HD ** -0.5
    zpad = jnp.zeros((lc, HD - lc), BF16) if lc < HD else None

    def conv_act(seg, h):
        lo = seg * width + h * HD
        acc = None
        for i in reversed(range(CONV_W)):
            term = ext_ref[pl.ds(pad - (CONV_W - 1) + i, tb), lo:lo + HD] * cw_ref[i:i + 1, lo:lo + HD]
            acc = term if acc is None else acc + term
        return _silu(acc)

    def l2n(x):
        return x * lax.rsqrt(jnp.sum(x * x, axis=-1, keepdims=True) + L2_EPS)

    units = []
    for h in range(nh):
        cols = slice(h * HD, (h + 1) * HD)
        q_all = l2n(conv_act(0, h)) * qscale
        k_all = l2n(conv_act(1, h))
        v_all = conv_act(2, h)
        for ci in range(tb // lc):
            rows = slice(ci * lc, (ci + 1) * lc)
            q = q_all[rows]
            k = k_all[rows]
            g_col = col_ref[rows, nh + h:nh + h + 1]
            b_col = col_ref[rows, 2 * nh + h:2 * nh + h + 1]
            g_row = row_ref[nh + h:nh + h + 1, rows]
            g_last = g_row[:, lc - 1:lc]
            kb = k * b_col
            qg_ref[rows, cols] = (q * jnp.exp(g_col)).astype(BF16)
            kd_ref[rows, cols] = (k * jnp.exp(g_last - g_col)).astype(BF16)
            units.append(dict(
                rows=rows, cols=cols,
                kbq=jnp.concatenate([kb, q], axis=0).astype(BF16),
                kbf=k.astype(BF16),
                decay=jnp.exp(jnp.where(tril, g_col - g_row, -jnp.inf)),
                rhs=jnp.concatenate([v_all[rows] * b_col, kb * jnp.exp(g_col)], axis=-1).astype(BF16)))

    for un in units:
        akq = _dot_nt(un["kbq"], un["kbf"])
        qkb = (akq[lc:] * un["decay"]).astype(BF16)
        if zpad is not None:
            qkb = jnp.concatenate([qkb, zpad], axis=-1)
        qk_ref[un["rows"], un["cols"]] = qkb
        un["p"] = jnp.where(strict, -(akq[:lc] * un["decay"]), 0.0)
        un["t"] = eye + un["p"]
    for un in units:
        pb = un["p"].astype(BF16)
        un["p"] = _dot(pb, pb)
    for _ in range(n_sq - 1):
        for un in units:
            st = _dot(jnp.concatenate([un["p"], un["t"]], axis=0).astype(BF16), un["p"].astype(BF16))
            un["p"] = st[:lc]
            un["t"] = un["t"] + st[lc:]
    for un in units:
        un["t"] = un["t"] + _dot(un["t"].astype(BF16), un["p"].astype(BF16))
    for un in units:
        uw = _dot(un["t"].astype(BF16), un["rhs"])
        u_ref[un["rows"], un["cols"]] = uw[:, :HD]
        w_ref[un["rows"], un["cols"]] = uw[:, HD:].astype(BF16)


def gdn_prep(raw, conv_state8, conv_w8, col, row, bsz, t_len, nh, lc):
    m = raw.shape[0]
    width = nh * HD
    c3 = 3 * width
    tb = _pick(t_len, (128, 64, 32, 16))
    nt = t_len // tb
    kern = functools.partial(_gdn_prep_kernel, nh=nh, lc=lc)
    blk = lambda b, t: (b * nt + t, 0)
    return pl.pallas_call(
        kern,
        grid=(bsz, nt),
        in_specs=[pl.BlockSpec((tb, c3), blk),
                  pl.BlockSpec((8, c3), lambda b, t: (jnp.maximum((b * nt + t) * (tb // 8) - 1, 0), 0)),
                  pl.BlockSpec((None, 8, c3), lambda b, t: (b, 0, 0)),
                  pl.BlockSpec((8, c3), lambda b, t: (0, 0)),
                  pl.BlockSpec((None, tb, LANES), lambda b, t: (b, t, 0)),
                  pl.BlockSpec((None, 32, tb), lambda b, t: (b, 0, t))],
        out_specs=[pl.BlockSpec((tb, width), blk) for _ in range(5)],
        out_shape=[jax.ShapeDtypeStruct((m, width), F32)] +
                  [jax.ShapeDtypeStruct((m, width), BF16) for _ in range(4)],
        scratch_shapes=[pltpu.VMEM((tb + 8, c3), F32)],
        compiler_params=_params(("parallel", "parallel")),
        name="gdn_prep",
    )(raw, raw, conv_state8, conv_w8, col, row)


def _gdn_scan_kernel(u_ref, w_ref, qg_ref, kd_ref, qk_ref, z_ref, row_ref, nw_ref, s0_ref,
                     o_ref, s_ref, *, nh, lc):
    t = pl.program_id(1)
    tb = u_ref.shape[0]
    nw = nw_ref[...]

    @pl.when(t == 0)
    def _():
        s_ref[...] = s0_ref[...]

    heads = [slice(h * HD, (h + 1) * HD) for h in range(nh)]
    states = [s_ref[h] for h in range(nh)]
    for ci in range(tb // lc):
        rows = slice(ci * lc, (ci + 1) * lc)
        wqs = [_dot(jnp.concatenate([w_ref[rows, cols], qg_ref[rows, cols]], axis=0), states[h].astype(BF16))
               for h, cols in enumerate(heads)]
        vbs = [(u_ref[rows, cols] - wqs[h][:lc]).astype(BF16) for h, cols in enumerate(heads)]
        o2s = [_dot(qk_ref[rows, h * HD:h * HD + lc], vbs[h]) for h in range(nh)]
        sds = [_dot_tn(kd_ref[rows, cols], vbs[h]) for h, cols in enumerate(heads)]
        for h, cols in enumerate(heads):
            g_last = row_ref[nh + h:nh + h + 1, ci * lc + lc - 1:ci * lc + lc]
            states[h] = states[h] * jnp.exp(g_last) + sds[h]
            o = wqs[h][lc:] + o2s[h]
            o = o * lax.rsqrt(jnp.mean(o * o, axis=-1, keepdims=True) + RMS_EPS) * nw * _silu(z_ref[rows, cols])
            o_ref[rows, cols] = o.astype(o_ref.dtype)
    for h in range(nh):
        s_ref[h] = states[h]


def gdn_scan(u, w, qg, kd, qk, z, row, norm_w, s0, bsz, t_len, nh, lc):
    m, width = u.shape
    tb = _pick(t_len, (256, 128, 64, 32, 16))
    nt = t_len // tb
    kern = functools.partial(_gdn_scan_kernel, nh=nh, lc=lc)
    blk = lambda b, t: (b * nt + t, 0)
    return pl.pallas_call(
        kern,
        grid=(bsz, nt),
        in_specs=[pl.BlockSpec((tb, width), blk) for _ in range(6)] +
                 [pl.BlockSpec((None, 32, tb), lambda b, t: (b, 0, t)),
                  pl.BlockSpec((1, HD), lambda b, t: (0, 0)),
                  pl.BlockSpec((None, nh, HD, HD), lambda b, t: (b, 0, 0, 0))],
        out_specs=[pl.BlockSpec((tb, width), blk),
                   pl.BlockSpec((None, nh, HD, HD), lambda b, t: (b, 0, 0, 0))],
        out_shape=[jax.ShapeDtypeStruct((m, width), BF16),
                   jax.ShapeDtypeStruct((bsz, nh, HD, HD), F32)],
        compiler_params=_params(("parallel", "arbitrary")),
        name="gdn_scan",
    )(u, w, qg, kd, qk, z, row, norm_w, s0)


def _out_ln_kernel(a_ref, b_ref, w_ref, res_ref, g_ref, beta_ref, o_ref, ob_ref, *, alpha):
    ka = a_ref.shape[1]
    y = _dot(a_ref[...], w_ref[0:ka, :]) + _dot(b_ref[...], w_ref[ka:, :])
    y = _ln_rows(alpha * res_ref[...] + y, g_ref[...], beta_ref[...])
    o_ref[...] = y
    ob_ref[...] = y.astype(BF16)


def out_proj_ln(mix_a, mix_b, w, res, g, beta, alpha):
    m, ka = mix_a.shape
    kb = mix_b.shape[1]
    d = w.shape[1]
    tm = _pick(m, (512, 256, 128, 64, 32, 16, 8))
    kern = functools.partial(_out_ln_kernel, alpha=alpha)
    row = lambda i: (i, 0)
    fixed = lambda i: (0, 0)
    return pl.pallas_call(
        kern,
        grid=(m // tm,),
        in_specs=[pl.BlockSpec((tm, ka), row), pl.BlockSpec((tm, kb), row),
                  pl.BlockSpec((ka + kb, d), fixed, pipeline_mode=pl.Buffered(1)),
                  pl.BlockSpec((tm, d), row),
                  pl.BlockSpec((1, d), fixed), pl.BlockSpec((1, d), fixed)],
        out_specs=[pl.BlockSpec((tm, d), row), pl.BlockSpec((tm, d), row)],
        out_shape=[jax.ShapeDtypeStruct((m, d), F32), jax.ShapeDtypeStruct((m, d), BF16)],
        compiler_params=_params(("parallel",)),
        name="out_proj_ln",
    )(mix_a, mix_b, w, res, g.reshape(1, d), beta.reshape(1, d))


def _ffn_up_kernel(*refs, cast_down):
    if cast_down:
        x_ref, wg_ref, wu_ref, wd_ref, h_ref, wdb_ref, wgb_ref, wub_ref = refs
    else:
        x_ref, wg_ref, wu_ref, h_ref, wgb_ref, wub_ref = refs

    @pl.when(pl.program_id(1) == 0)
    def _():
        for r0 in range(0, wg_ref.shape[0], CAST_ROWS):
            rows = slice(r0, r0 + CAST_ROWS)
            wgb_ref[rows, :] = wg_ref[rows, :].astype(BF16)
            wub_ref[rows, :] = wu_ref[rows, :].astype(BF16)
        if cast_down:
            wdb_ref[...] = wd_ref[...].astype(BF16)

    x = x_ref[...]
    gate = _dot(x, wgb_ref[...])
    up = _dot(x, wub_ref[...])
    h_ref[...] = (_silu(gate) * up).astype(h_ref.dtype)


def ffn_up(xb, wg, wu, layer, wd=None):
    m, d = xb.shape
    f = wg.shape[2]
    assert d % CAST_ROWS == 0
    tm = _pick(m, (1024, 512, 256, 128, 64, 32, 16, 8))
    tf = _pick(f, (512, 256, 128))
    wspec = pl.BlockSpec((None, d, tf), lambda j, i: (layer, 0, j))
    in_specs = [pl.BlockSpec((tm, d), lambda j, i: (i, 0)), wspec, wspec]
    out_specs = [pl.BlockSpec((tm, tf), lambda j, i: (i, j))]
    out_shape = [jax.ShapeDtypeStruct((m, f), BF16)]
    args = [xb, wg, wu]
    if wd is not None:
        dout = wd.shape[2]
        in_specs.append(pl.BlockSpec((None, tf, dout), lambda j, i: (layer, j, 0)))
        out_specs.append(pl.BlockSpec((tf, dout), lambda j, i: (j, 0)))
        out_shape.append(jax.ShapeDtypeStruct((f, dout), BF16))
        args.append(wd)
    return pl.pallas_call(
        functools.partial(_ffn_up_kernel, cast_down=wd is not None),
        grid=(f // tf, m // tm),
        in_specs=in_specs,
        out_specs=out_specs,
        out_shape=out_shape,
        scratch_shapes=[pltpu.VMEM((d, tf), BF16), pltpu.VMEM((d, tf), BF16)],
        compiler_params=_params(("parallel", "arbitrary")),
        name="ffn_up",
    )(*args)


def _ffn_down_kernel(h_ref, w_ref, res_ref, g_ref, beta_ref, o_ref, ob_ref, *, alpha):
    y = _dot(h_ref[...], w_ref[...])
    y = _ln_rows(alpha * res_ref[...] + y, g_ref[...], beta_ref[...])
    o_ref[...] = y
    ob_ref[...] = y.astype(BF16)


def ffn_down_ln(h, w, res, g, beta, alpha):
    m, f = h.shape
    d = w.shape[1]
    tm = _pick(m, (256, 128, 64, 32, 16, 8))
    kern = functools.partial(_ffn_down_kernel, alpha=alpha)
    row = lambda i: (i, 0)
    fixed = lambda i: (0, 0)
    return pl.pallas_call(
        kern,
        grid=(m // tm,),
        in_specs=[pl.BlockSpec((tm, f), row),
                  pl.BlockSpec((f, d), fixed, pipeline_mode=pl.Buffered(1)),
                  pl.BlockSpec((tm, d), row),
                  pl.BlockSpec((1, d), fixed), pl.BlockSpec((1, d), fixed)],
        out_specs=[pl.BlockSpec((tm, d), row), pl.BlockSpec((tm, d), row)],
        out_shape=[jax.ShapeDtypeStruct((m, d), F32), jax.ShapeDtypeStruct((m, d), BF16)],
        compiler_params=_params(("parallel",)),
        name="ffn_down_ln",
    )(h, w, res, g.reshape(1, d), beta.reshape(1, d))


def _layer(x, xb, bsz, t_len, layer, wts, lw, fox_cache, conv_state, s0, kv_prev, alpha):
    nh = lw["nh"]
    width = nh * HD
    lc = min(t_len, CHUNK)
    w_in = wts["w_in"]
    k_prev, v_prev = kv_prev if kv_prev is not None else (None, None)
    qb, = proj_segments(xb, w_in, layer, 0, 1, 0, (BF16,), out_scale=HD ** -0.5)
    kb, k_stack = proj_segments(xb, w_in, layer, 1, 1, 0, (BF16,), prev=k_prev, stack=True)
    vb, v_stack = proj_segments(xb, w_in, layer, 2, 1, 0, (BF16,), prev=v_prev, stack=True)
    raw, = proj_segments(xb, w_in, layer, 3, 3, nh, (F32,))
    z, = proj_segments(xb, w_in, layer, 6, 1, 3 * nh, (F32,))
    logf, col, row, qa, ka = gates(xb, w_in, layer, lw["gate_bias"], lw["alog_row"], bsz, t_len, lc, nh)

    if fox_cache is None:
        mix_f = fox_prompt(qb, qa, kb, ka, vb, lw["fox_norm_w"], bsz, t_len, nh)
    else:
        k_cache, v_cache, lf_cache = fox_cache
        mix_f = fox_sample(qb, kb, vb, k_cache, v_cache, layer, jnp.transpose(lf_cache, (0, 2, 1)), col, row,
                           lw["fox_norm_w"], bsz, t_len, nh)

    conv_state8 = jnp.pad(conv_state, ((0, 0), (8 - (CONV_W - 1), 0), (0, 0)))
    u, w, qg, kd, qk = gdn_prep(raw, conv_state8, lw["conv_w8"], col, row, bsz, t_len, nh, lc)
    mix_g, s_fin = gdn_scan(u, w, qg, kd, qk, z, row, lw["gdn_norm_w"], s0, bsz, t_len, nh, lc)

    x1, x1b = out_proj_ln(mix_f, mix_g, lw["w_out"], x, lw["ln_mix_g"], lw["ln_mix_b"], alpha)
    if "w_down" in lw:
        hmid, = ffn_up(x1b, wts["w_gate"], wts["w_up"], layer)
    else:
        hmid, lw["w_down"] = ffn_up(x1b, wts["w_gate"], wts["w_up"], layer, wd=wts["w_down"])
    x2, x2b = ffn_down_ln(hmid, lw["w_down"], x1, lw["ln_ffn_g"], lw["ln_ffn_b"], alpha)

    new_buf = raw.reshape(bsz, t_len, -1)[:, t_len - (CONV_W - 1):, :]
    return x2, x2b, (k_stack, v_stack), (logf, s_fin, new_buf)


def kernel(x_prompt, x_sample, cache_fox_k, cache_fox_v, cache_fox_logf, state_gdn, state_gdn_conv, ln_in_g, ln_in_b, w_in, fox_f_bias, fox_norm_w, gdn_conv_w, gdn_a_log, gdn_dt_bias, gdn_norm_w, w_out, ln_mix_g, ln_mix_b, ffn_w_gate, ffn_w_up, ffn_w_down, ln_ffn_g, ln_ffn_b):
    depth = w_in.shape[0]
    bp, tp, d = x_prompt.shape
    bs, ts, _ = x_sample.shape
    nh = fox_f_bias.shape[1]
    assert fox_norm_w.shape[1] == HD and gdn_norm_w.shape[1] == HD and gdn_a_log.shape[1] == nh
    width = nh * HD
    assert width == SEG and w_in.shape[2] == 7 * width + 3 * nh and gdn_conv_w.shape[2] == 3 * width
    alpha = (2 * depth) ** 0.25

    xp, xpb = layer_norm_rows(x_prompt.reshape(bp * tp, d), ln_in_g, ln_in_b)
    xs, xsb = layer_norm_rows(x_sample.reshape(bs * ts, d), ln_in_g, ln_in_b)

    zero_conv = jnp.zeros((bp, CONV_W - 1, 3 * width), F32)
    zero_state = jnp.zeros((bp, nh, HD, HD), F32)
    k_cache, v_cache = cache_fox_k, cache_fox_v
    wts = dict(w_in=w_in, w_gate=ffn_w_gate, w_up=ffn_w_up, w_down=ffn_w_down)
    kv_p = kv_s = None
    ps, ss = [], []
    for l in range(depth):
        gate_bias = jnp.concatenate([fox_f_bias[l], gdn_dt_bias[l], jnp.zeros((LANES - 2 * nh,), F32)]).reshape(1, LANES)
        alog_row = jnp.concatenate([jnp.zeros((nh,), F32), gdn_a_log[l], jnp.zeros((LANES - 2 * nh,), F32)]).reshape(1, LANES)
        lw = dict(
            nh=nh,
            gate_bias=gate_bias, alog_row=alog_row,
            fox_norm_w=fox_norm_w[l].reshape(1, HD),
            gdn_norm_w=gdn_norm_w[l].reshape(1, HD),
            conv_w8=jnp.pad(gdn_conv_w[l], ((0, 8 - CONV_W), (0, 0))),
            w_out=cast_bf16(w_out, l),
            ln_mix_g=ln_mix_g[l], ln_mix_b=ln_mix_b[l],
            ln_ffn_g=ln_ffn_g[l], ln_ffn_b=ln_ffn_b[l],
        )
        xp, xpb, kv_p, st_p = _layer(xp, xpb, bp, tp, l, wts, lw, None, zero_conv, zero_state, kv_p, alpha)
        xs, xsb, kv_s, st_s = _layer(xs, xsb, bs, ts, l, wts, lw, (k_cache, v_cache, cache_fox_logf[l]),
                                     state_gdn_conv[l], state_gdn[l], kv_s, alpha)
        ps.append(st_p)
        ss.append(st_s)

    def stacked(states, i):
        return jnp.stack([st[i] for st in states], axis=0)

    def heads(stack, bsz, t_len):
        return stack.reshape(depth, bsz, t_len, nh, HD)

    return (xp.reshape(bp, tp, d), xs.reshape(bs, ts, d),
            heads(kv_p[0], bp, tp), heads(kv_p[1], bp, tp), stacked(ps, 0), stacked(ps, 1), stacked(ps, 2),
            heads(kv_s[0], bs, ts), heads(kv_s[1], bs, ts), stacked(ss, 0), stacked(ss, 1), stacked(ss, 2))
```

```python
import functools
import math

import jax
import jax.numpy as jnp
from jax import lax
from jax.experimental import pallas as pl
from jax.experimental.pallas import tpu as pltpu

LN_EPS = 1e-5
RMS_EPS = 1e-6
L2_EPS = 1e-6
CHUNK = 64
CONV_W = 4
HD = 128
LANES = 128
VMEM_LIMIT = 56 * 1024 * 1024

F32 = jnp.float32
BF16 = jnp.bfloat16


def _pick(dim, cands):
    for c in cands:
        if c <= dim and dim % c == 0:
            return c
    return dim


def _params(sem):
    return pltpu.CompilerParams(dimension_semantics=sem, vmem_limit_bytes=VMEM_LIMIT)


def _split3(x):
    hi = x.astype(BF16)
    r1 = x - hi.astype(F32)
    mid = r1.astype(BF16)
    lo = (r1 - mid.astype(F32)).astype(BF16)
    return hi, mid, lo


def _dot(a, b):
    return jnp.dot(a, b, preferred_element_type=F32)


def _dot_nt(a, b):
    return lax.dot_general(a, b, (((1,), (1,)), ((), ())), preferred_element_type=F32)


def _dot_tn(a, b):
    return lax.dot_general(a, b, (((0,), (0,)), ((), ())), preferred_element_type=F32)


def _ln_rows(y, g, b):
    mu = jnp.mean(y, axis=-1, keepdims=True)
    d = y - mu
    var = jnp.mean(d * d, axis=-1, keepdims=True)
    return d * lax.rsqrt(var + LN_EPS) * g + b


def _silu(x):
    return x / (1.0 + jnp.exp(-x))


def _ln_kernel(x_ref, g_ref, b_ref, o_ref, ob_ref):
    y = _ln_rows(x_ref[...], g_ref[...], b_ref[...])
    o_ref[...] = y
    ob_ref[...] = y.astype(BF16)


def layer_norm_rows(x, g, b):
    m, d = x.shape
    tm = _pick(m, (512, 256, 128, 64, 32, 16, 8))
    return pl.pallas_call(
        _ln_kernel,
        grid=(m // tm,),
        in_specs=[pl.BlockSpec((tm, d), lambda i: (i, 0)),
                  pl.BlockSpec((1, d), lambda i: (0, 0)),
                  pl.BlockSpec((1, d), lambda i: (0, 0))],
        out_specs=[pl.BlockSpec((tm, d), lambda i: (i, 0)),
                   pl.BlockSpec((tm, d), lambda i: (i, 0))],
        out_shape=[jax.ShapeDtypeStruct((m, d), F32), jax.ShapeDtypeStruct((m, d), BF16)],
        compiler_params=_params(("parallel",)),
        name="ln_in",
    )(x, g.reshape(1, d), b.reshape(1, d))


def _cast_kernel(w_ref, o_ref):
    o_ref[...] = w_ref[...].astype(o_ref.dtype)


def cast_bf16(w, layer):
    _, k, n = w.shape
    tk = _pick(k, (512, 256, 128, 64, 32, 16))
    return pl.pallas_call(
        _cast_kernel,
        grid=(k // tk,),
        in_specs=[pl.BlockSpec((None, tk, n), lambda i: (layer, i, 0))],
        out_specs=pl.BlockSpec((tk, n), lambda i: (i, 0)),
        out_shape=jax.ShapeDtypeStruct((k, n), BF16),
        compiler_params=_params(("parallel",)),
        name="cast_bf16",
    )(w)


HEAD_GROUP = 4
SEG = 1024
CAST_ROWS = 256


def _proj_kernel(*refs, shift, n_out, n_prev, stack, out_scale):
    it = iter(refs)
    x_ref = next(it)
    wm_ref = next(it)
    wn_ref = next(it) if shift else None
    prev_ref = next(it) if n_prev else None
    o_refs = [next(it) for _ in range(n_out)]
    stack_ref = next(it) if stack else None
    wb_ref = next(it)

    @pl.when(pl.program_id(1) == 0)
    def _():
        for c0 in range(0, wm_ref.shape[1], CAST_ROWS):
            cols = slice(c0, c0 + CAST_ROWS)
            wm = wm_ref[:, cols]
            if shift:
                wm = jnp.concatenate([wm[shift:, :], wn_ref[0:shift, cols]], axis=0)
            wb_ref[cols, :] = wm.T.astype(BF16)

    acc = _dot(x_ref[...], wb_ref[...])
    for o in o_refs:
        o[...] = (acc if out_scale is None else acc * out_scale).astype(o.dtype)
    if stack:
        for l in range(n_prev):
            stack_ref[l] = prev_ref[l]
        stack_ref[n_prev] = acc.reshape(acc.shape[0], SEG // HD, HD)


def proj_segments(xb, wt, layer, seg0, nseg, shift, out_dtypes, prev=None, stack=False, out_scale=None):
    m, k = xb.shape
    assert k % CAST_ROWS == 0 and not (stack and nseg != 1) and shift % 8 == 0
    nb = 8 if shift <= 8 else 32
    assert shift <= nb and SEG % nb == 0
    n_prev = 0 if prev is None else prev.shape[0]
    tm = _pick(m, (1024, 512, 256, 128, 64, 32, 16, 8))
    single = pl.Buffered(1) if nseg == 1 else None
    in_specs = [pl.BlockSpec((tm, k), lambda j, i: (i, 0)),
                pl.BlockSpec((None, SEG, k), lambda j, i: (layer, seg0 + j, 0), pipeline_mode=single)]
    args = [xb, wt]
    if shift:
        in_specs.append(pl.BlockSpec((None, nb, k), lambda j, i: (layer, (seg0 + j + 1) * (SEG // nb), 0),
                                     pipeline_mode=single))
        args.append(wt)
    if n_prev:
        in_specs.append(pl.BlockSpec((n_prev, tm, SEG // HD, HD), lambda j, i: (0, i, 0, 0)))
        args.append(prev)
    out_specs = [pl.BlockSpec((tm, SEG), lambda j, i: (i, j)) for _ in out_dtypes]
    out_shape = [jax.ShapeDtypeStruct((m, nseg * SEG), dt) for dt in out_dtypes]
    if stack:
        out_specs.append(pl.BlockSpec((n_prev + 1, tm, SEG // HD, HD), lambda j, i: (0, i, 0, 0)))
        out_shape.append(jax.ShapeDtypeStruct((n_prev + 1, m, SEG // HD, HD), F32))
    kern = functools.partial(_proj_kernel, shift=shift, n_out=len(out_dtypes), n_prev=n_prev, stack=stack,
                             out_scale=out_scale)
    return pl.pallas_call(
        kern,
        grid=(nseg, m // tm),
        in_specs=in_specs,
        out_specs=out_specs,
        out_shape=out_shape,
        scratch_shapes=[pltpu.VMEM((k, SEG), BF16)],
        compiler_params=_params(("parallel", "arbitrary")),
        name="proj",
    )(*args)


def _gates_kernel(x_ref, wf_ref, wa_ref, wb_ref, bias_ref, alog_ref, logf_ref, col_ref, row_ref, qa_ref, ka_ref,
                  carry_ref, *, sb, chunk, nh):
    t = pl.program_id(1)
    tb = x_ref.shape[0]

    @pl.when(t == 0)
    def _():
        carry_ref[...] = jnp.zeros_like(carry_ref)

    lane = lax.broadcasted_iota(jnp.int32, (1, LANES), 1)
    wt = jnp.concatenate([wf_ref[...], wa_ref[...], wb_ref[...],
                          jnp.zeros((LANES - 3 * nh, wf_ref.shape[1]), F32)], axis=0).astype(BF16)
    raw = _dot_nt(x_ref[...], wt) + bias_ref[...]
    sp_neg = jnp.maximum(-raw, 0.0) + jnp.log1p(jnp.exp(-jnp.abs(raw)))
    sp_pos = jnp.maximum(raw, 0.0) + jnp.log1p(jnp.exp(-jnp.abs(raw)))
    logf = -sp_neg
    g = -jnp.exp(alog_ref[...]) * sp_pos
    beta = 1.0 / (1.0 + jnp.exp(-raw))
    is_f = lane < nh
    is_g = (lane >= nh) & (lane < 2 * nh)
    is_b = (lane >= 2 * nh) & (lane < 3 * nh)
    logf_ref[...] = logf[:, :nh]

    r = lax.broadcasted_iota(jnp.int32, (sb, sb), 0)
    c = lax.broadcasted_iota(jnp.int32, (sb, sb), 1)
    tri_full = (r >= c).astype(BF16)
    tri_chunk = ((r >= c) & ((r // chunk) == (c // chunk))).astype(BF16)
    er = lax.broadcasted_iota(jnp.int32, (32, LANES), 0)
    ec = lax.broadcasted_iota(jnp.int32, (32, LANES), 1)
    sel = (er == ec).astype(BF16)
    wide = nh * HD
    gr = lax.broadcasted_iota(jnp.int32, (LANES, wide), 0)
    gc = lax.broadcasted_iota(jnp.int32, (LANES, wide), 1)
    gh, gj = gr % nh, gr // nh
    spread_q = ((gc == gh * HD + 3 + gj) & (gj < 3)).astype(BF16)
    spread_k = ((gc == gh * HD + gj) & (gj < 3)).astype(BF16)
    lane_in_head = lax.broadcasted_iota(jnp.int32, (1, wide), 1) % HD
    ones_q = (lane_in_head < 3).astype(F32)
    ones_k = ((lane_in_head >= 3) & (lane_in_head < 6)).astype(F32)

    carry = carry_ref[...]
    for s in range(tb // sb):
        rows = slice(s * sb, (s + 1) * sb)
        lf = jnp.where(is_f, logf[rows], 0.0)
        gg = jnp.where(is_g, g[rows], 0.0)
        cs = jnp.zeros((sb, LANES), F32)
        for part in _split3(lf):
            cs = cs + _dot(tri_full, part)
        for part in _split3(gg):
            cs = cs + _dot(tri_chunk, part)
        cs = cs + carry
        carry = jnp.where(is_f, cs[sb - 1:sb, :], 0.0)
        colv = jnp.where(is_b, beta[rows], cs)
        col_ref[rows, :] = colv
        rv = jnp.zeros((32, sb), F32)
        terms = jnp.zeros((sb, LANES), F32)
        for j, part in enumerate(_split3(colv)):
            rv = rv + _dot_nt(sel, part)
            pf = jnp.where(is_f, part.astype(F32), 0.0)
            terms = terms + (pf if j == 0 else pltpu.roll(pf, j * nh, 1))
        tb16 = terms.astype(BF16)
        row_ref[:, rows] = rv
        qa_ref[rows, :] = (ones_q + _dot(tb16, spread_q)).astype(BF16)
        ka_ref[rows, :] = (ones_k - _dot(tb16, spread_k)).astype(BF16)
    carry_ref[...] = carry


def gates(xb, wt, layer, bias_row, alog_row, bsz, t_len, chunk, nh):
    m, d = xb.shape
    tb = _pick(t_len, (512, 256, 128, 64, 32, 16))
    sb = min(tb, 128)
    nt = t_len // tb
    width = nh * HD
    assert nh == 8 and 3 * nh <= LANES
    f_blk = 3 * width // nh
    a_blk = (6 * width + nh) // nh
    kern = functools.partial(_gates_kernel, sb=sb, chunk=chunk, nh=nh)
    return pl.pallas_call(
        kern,
        grid=(bsz, nt),
        in_specs=[pl.BlockSpec((tb, d), lambda b, t: (b * nt + t, 0)),
                  pl.BlockSpec((None, nh, d), lambda b, t: (layer, f_blk, 0)),
                  pl.BlockSpec((None, nh, d), lambda b, t: (layer, a_blk, 0)),
                  pl.BlockSpec((None, nh, d), lambda b, t: (layer, a_blk + 1, 0)),
                  pl.BlockSpec((1, LANES), lambda b, t: (0, 0)),
                  pl.BlockSpec((1, LANES), lambda b, t: (0, 0))],
        out_specs=[pl.BlockSpec((None, tb, nh), lambda b, t: (b, t, 0)),
                   pl.BlockSpec((None, tb, LANES), lambda b, t: (b, t, 0)),
                   pl.BlockSpec((None, 32, tb), lambda b, t: (b, 0, t)),
                   pl.BlockSpec((tb, width), lambda b, t: (b * nt + t, 0)),
                   pl.BlockSpec((tb, width), lambda b, t: (b * nt + t, 0))],
        out_shape=[jax.ShapeDtypeStruct((bsz, t_len, nh), F32),
                   jax.ShapeDtypeStruct((bsz, t_len, LANES), F32),
                   jax.ShapeDtypeStruct((bsz, 32, t_len), F32),
                   jax.ShapeDtypeStruct((m, width), BF16),
                   jax.ShapeDtypeStruct((m, width), BF16)],
        scratch_shapes=[pltpu.VMEM((1, LANES), F32)],
        compiler_params=_params(("parallel", "arbitrary")),
        name="gates",
    )(xb, wt, wt, wt, bias_row, alog_row)


def _fox_prompt_kernel(q_ref, qa_ref, k_ref, ka_ref, v_ref, nw_ref, o_ref, q2_ref, m_ref, acc_ref, *, nh, tq):
    qi = pl.program_id(1)
    nw = nw_ref[...]
    r = lax.broadcasted_iota(jnp.int32, (tq, LANES), 0)
    c = lax.broadcasted_iota(jnp.int32, (tq, LANES), 1)
    heads = [slice(h * HD, (h + 1) * HD) for h in range(nh)]
    for h, cols in enumerate(heads):
        q2_ref[h] = jnp.concatenate([q_ref[:, cols], qa_ref[:, cols]], axis=-1)

    def scores(start, width, group, masked):
        out = []
        for h in group:
            cols = heads[h]
            k2 = jnp.concatenate([k_ref[pl.ds(start, width), cols], ka_ref[pl.ds(start, width), cols]], axis=-1)
            s = _dot_nt(q2_ref[h], k2)
            pieces = [s[:, k * LANES:(k + 1) * LANES] for k in range(width // LANES)]
            if masked:
                pieces = [jnp.where(r >= c + k * LANES, sk, -jnp.inf) for k, sk in enumerate(pieces)]
            out.append(pieces)
        return out

    def row_max(pieces):
        mx = pieces[0]
        for sk in pieces[1:]:
            mx = jnp.maximum(mx, sk)
        return jnp.broadcast_to(jnp.max(mx, axis=-1, keepdims=True), (tq, LANES))

    def pv(p_pieces, start, width, h):
        p = jnp.concatenate(p_pieces, axis=-1).astype(BF16)
        v_aug = jnp.concatenate([v_ref[pl.ds(start, width), heads[h]], jnp.ones((width, LANES), BF16)], axis=-1)
        return _dot(p, v_aug)

    groups = [list(range(g, min(g + HEAD_GROUP, nh))) for g in range(0, nh, HEAD_GROUP)]

    def block(start, width, first):
        pending = scores(start, width, groups[0], first)
        for gi, group in enumerate(groups):
            ss = pending
            if gi + 1 < len(groups):
                pending = scores(start, width, groups[gi + 1], first)
            ps, alphas = [], []
            for h, pieces in zip(group, ss):
                mx = row_max(pieces)
                if first:
                    m_new = mx
                else:
                    m = m_ref[h]
                    m_new = jnp.maximum(m, mx)
                    alphas.append(jnp.exp(m - m_new))
                m_ref[h] = m_new
                ps.append([jnp.exp(sk - m_new) for sk in pieces])
            for i, (h, pp) in enumerate(zip(group, ps)):
                if first:
                    acc_ref[h] = pv(pp, start, width, h)
                else:
                    a = alphas[i]
                    acc_ref[h] = jnp.concatenate([a, a], axis=-1) * acc_ref[h] + pv(pp, start, width, h)

    block(pl.multiple_of(qi * tq, tq), tq, True)

    @pl.when(qi % 2 == 1)
    def _():
        block(pl.multiple_of((qi - 1) * tq, tq), tq, False)

    def step(j, carry):
        block(pl.multiple_of(j * (2 * tq), 2 * tq), 2 * tq, False)
        return carry

    lax.fori_loop(0, qi // 2, step, 0)
    for h in range(nh):
        acc = acc_ref[h]
        o = acc[:, :HD] / acc[:, HD:]
        o = o * lax.rsqrt(jnp.mean(o * o, axis=-1, keepdims=True) + RMS_EPS) * nw
        o_ref[:, heads[h]] = o.astype(o_ref.dtype)


def fox_prompt(qb, qa, kb, ka, vb, norm_w, bsz, t_len, nh):
    tq = _pick(t_len, (256, 128))
    assert tq % LANES == 0 and t_len % tq == 0
    nq = t_len // tq
    w = nh * HD
    kern = functools.partial(_fox_prompt_kernel, nh=nh, tq=tq)
    qspec = pl.BlockSpec((tq, w), lambda b, i: (b * nq + i, 0))
    kspec = pl.BlockSpec((t_len, w), lambda b, i: (b, 0))
    return pl.pallas_call(
        kern,
        grid=(bsz, nq),
        in_specs=[qspec, qspec, kspec, kspec, kspec, pl.BlockSpec((1, HD), lambda b, i: (0, 0))],
        out_specs=pl.BlockSpec((tq, w), lambda b, i: (b * nq + i, 0)),
        out_shape=jax.ShapeDtypeStruct((bsz * t_len, w), BF16),
        scratch_shapes=[pltpu.VMEM((nh, tq, 2 * HD), BF16), pltpu.VMEM((nh, tq, LANES), F32),
                        pltpu.VMEM((nh, tq, 2 * HD), F32)],
        compiler_params=_params(("parallel", "parallel")),
        name="fox_prompt",
    )(qb, qa, kb, ka, vb, norm_w)


def _cache_gate_kernel(lf_ref, o_ref):
    past = lf_ref.shape[1]
    r = lax.broadcasted_iota(jnp.int32, (past, past), 0)
    c = lax.broadcasted_iota(jnp.int32, (past, past), 1)
    later = (r > c).astype(BF16)
    acc = jnp.zeros(lf_ref.shape, F32)
    for part in _split3(lf_ref[...]):
        acc = acc + _dot(part, later)
    o_ref[...] = acc


def cache_gate_suffix(lf_t):
    bsz, nh, past = lf_t.shape
    rows = bsz * nh
    return pl.pallas_call(
        _cache_gate_kernel,
        grid=(1,),
        in_specs=[pl.BlockSpec((rows, past), lambda i: (0, 0))],
        out_specs=pl.BlockSpec((rows, past), lambda i: (0, 0)),
        out_shape=jax.ShapeDtypeStruct((rows, past), F32),
        compiler_params=_params(("arbitrary",)),
        name="cache_gate_suffix",
    )(lf_t.reshape(rows, past)).reshape(bsz, nh, past)


def _fox_sample_kernel(q_ref, kn_ref, vn_ref, kc_ref, vc_ref, sfx_ref, col_ref, row_ref, nw_ref, o_ref, *, nh):
    t_len = q_ref.shape[0]
    past = kc_ref.shape[0]
    nw = nw_ref[...]
    heads = [slice(h * HD, (h + 1) * HD) for h in range(nh)]
    q_all = jnp.concatenate([q_ref[:, cols] for cols in heads], axis=0)
    cn_all = jnp.concatenate([col_ref[:, h:h + 1] for h in range(nh)], axis=0)
    k_flat = kc_ref[...].reshape(past * nh, HD).astype(BF16)
    v_flat = vc_ref[...].reshape(past * nh, HD).astype(BF16)
    rr = lax.broadcasted_iota(jnp.int32, (nh * t_len, past * nh), 0)
    cc = lax.broadcasted_iota(jnp.int32, (nh * t_len, past * nh), 1)
    own_head = (cc % nh) == (rr // t_len)
    s_c = jnp.where(own_head, _dot_nt(q_all, k_flat) + (cn_all + sfx_ref[...]), -jnp.inf)
    tr = lax.broadcasted_iota(jnp.int32, (t_len, t_len), 0)
    tc = lax.broadcasted_iota(jnp.int32, (t_len, t_len), 1)
    causal = tr >= tc
    s_n = []
    for h, cols in enumerate(heads):
        sn = _dot_nt(q_ref[:, cols], kn_ref[:, cols]) + (col_ref[:, h:h + 1] - row_ref[h:h + 1, :])
        s_n.append(jnp.where(causal, sn, -jnp.inf))
    m_n = jnp.concatenate([jnp.max(sn, axis=-1, keepdims=True) for sn in s_n], axis=0)
    m = jnp.maximum(jnp.max(s_c, axis=-1, keepdims=True), m_n)
    p_c = jnp.exp(s_c - m)
    l_c = jnp.sum(p_c, axis=-1, keepdims=True)
    o_c = _dot(p_c.astype(BF16), v_flat)
    for h, cols in enumerate(heads):
        rows = slice(h * t_len, (h + 1) * t_len)
        p_n = jnp.exp(s_n[h] - m[rows])
        l = l_c[rows] + jnp.sum(p_n, axis=-1, keepdims=True)
        o = (o_c[rows] + _dot(p_n.astype(BF16), vn_ref[:, cols])) / l
        o = o * lax.rsqrt(jnp.mean(o * o, axis=-1, keepdims=True) + RMS_EPS) * nw
        o_ref[:, cols] = o.astype(o_ref.dtype)


def fox_sample(qb, knb, vnb, k_cache, v_cache, layer, suffix_flat, col, row, norm_w, bsz, t_len, nh):
    w = nh * HD
    past = k_cache.shape[2]
    kern = functools.partial(_fox_sample_kernel, nh=nh)
    return pl.pallas_call(
        kern,
        grid=(bsz,),
        in_specs=[pl.BlockSpec((t_len, w), lambda b: (b, 0)),
                  pl.BlockSpec((t_len, w), lambda b: (b, 0)),
                  pl.BlockSpec((t_len, w), lambda b: (b, 0)),
                  pl.BlockSpec((None, None, past, nh, HD), lambda b: (layer, b, 0, 0, 0)),
                  pl.BlockSpec((None, None, past, nh, HD), lambda b: (layer, b, 0, 0, 0)),
                  pl.BlockSpec((None, 1, past * nh), lambda b: (b, 0, 0)),
                  pl.BlockSpec((None, t_len, LANES), lambda b: (b, 0, 0)),
                  pl.BlockSpec((None, 32, t_len), lambda b: (b, 0, 0)),
                  pl.BlockSpec((1, HD), lambda b: (0, 0))],
        out_specs=pl.BlockSpec((t_len, w), lambda b: (b, 0)),
        out_shape=jax.ShapeDtypeStruct((bsz * t_len, w), BF16),
        compiler_params=_params(("parallel",)),
        name="fox_sample",
    )(qb, knb, vnb, k_cache, v_cache, suffix_flat, col, row, norm_w)


def _gdn_prep_kernel(raw_ref, prev_ref, cbuf_ref, cw_ref, col_ref, row_ref,
                     u_ref, w_ref, qg_ref, kd_ref, qk_ref, ext_ref, *, nh, lc):
    t = pl.program_id(1)
    tb = raw_ref.shape[0]
    width = nh * HD
    pad = 8
    @pl.when(t == 0)
    def _():
        ext_ref[0:pad, :] = cbuf_ref[...]

    @pl.when(t > 0)
    def _():
        ext_ref[0:pad, :] = prev_ref[...]

    ext_ref[pad:pad + tb, :] = raw_ref[...]

    r = lax.broadcasted_iota(jnp.int32, (lc, lc), 0)
    c = lax.broadcasted_iota(jnp.int32, (lc, lc), 1)
    tril = r >= c
    strict = r > c
    eye = (r == c).astype(F32)
    n_sq = int(math.log2(lc)) - 1
    qscale = HD ** -0.5
    zpad = jnp.zeros((lc, HD - lc), BF16) if lc < HD else None

    def conv_act(seg, h):
        lo = seg * width + h * HD
        acc = None
        for i in reversed(range(CONV_W)):
            term = ext_ref[pl.ds(pad - (CONV_W - 1) + i, tb), lo:lo + HD] * cw_ref[i:i + 1, lo:lo + HD]
            acc = term if acc is None else acc + term
        return _silu(acc)

    def l2n(x):
        return x * lax.rsqrt(jnp.sum(x * x, axis=-1, keepdims=True) + L2_EPS)

    units = []
    for h in range(nh):
        cols = slice(h * HD, (h + 1) * HD)
        q_all = l2n(conv_act(0, h)) * qscale
        k_all = l2n(conv_act(1, h))
        v_all = conv_act(2, h)
        for ci in range(tb // lc):
            rows = slice(ci * lc, (ci + 1) * lc)
            q = q_all[rows]
            k = k_all[rows]
            g_col = col_ref[rows, nh + h:nh + h + 1]
            b_col = col_ref[rows, 2 * nh + h:2 * nh + h + 1]
            g_row = row_ref[nh + h:nh + h + 1, rows]
            g_last = g_row[:, lc - 1:lc]
            kb = k * b_col
            qg_ref[rows, cols] = (q * jnp.exp(g_col)).astype(BF16)
            kd_ref[rows, cols] = (k * jnp.exp(g_last - g_col)).astype(BF16)
            units.append(dict(
                rows=rows, cols=cols,
                kbq=jnp.concatenate([kb, q], axis=0).astype(BF16),
                kbf=k.astype(BF16),
                decay=jnp.exp(jnp.where(tril, g_col - g_row, -jnp.inf)),
                rhs=jnp.concatenate([v_all[rows] * b_col, kb * jnp.exp(g_col)], axis=-1).astype(BF16)))

    for un in units:
        akq = _dot_nt(un["kbq"], un["kbf"])
        qkb = (akq[lc:] * un["decay"]).astype(BF16)
        if zpad is not None:
            qkb = jnp.concatenate([qkb, zpad], axis=-1)
        qk_ref[un["rows"], un["cols"]] = qkb
        un["p"] = jnp.where(strict, -(akq[:lc] * un["decay"]), 0.0)
        un["t"] = eye + un["p"]
    for un in units:
        pb = un["p"].astype(BF16)
        un["p"] = _dot(pb, pb)
    for _ in range(n_sq - 1):
        for un in units:
            st = _dot(jnp.concatenate([un["p"], un["t"]], axis=0).astype(BF16), un["p"].astype(BF16))
            un["p"] = st[:lc]
            un["t"] = un["t"] + st[lc:]
    for un in units:
        un["t"] = un["t"] + _dot(un["t"].astype(BF16), un["p"].astype(BF16))
    for un in units:
        uw = _dot(un["t"].astype(BF16), un["rhs"])
        u_ref[un["rows"], un["cols"]] = uw[:, :HD]
        w_ref[un["rows"], un["cols"]] = uw[:, HD:].astype(BF16)


def gdn_prep(raw, conv_state8, conv_w8, col, row, bsz, t_len, nh, lc):
    m = raw.shape[0]
    width = nh * HD
    c3 = 3 * width
    tb = _pick(t_len, (128, 64, 32, 16))
    nt = t_len // tb
    kern = functools.partial(_gdn_prep_kernel, nh=nh, lc=lc)
    blk = lambda b, t: (b * nt + t, 0)
    return pl.pallas_call(
        kern,
        grid=(bsz, nt),
        in_specs=[pl.BlockSpec((tb, c3), blk),
                  pl.BlockSpec((8, c3), lambda b, t: (jnp.maximum((b * nt + t) * (tb // 8) - 1, 0), 0)),
                  pl.BlockSpec((None, 8, c3), lambda b, t: (b, 0, 0)),
                  pl.BlockSpec((8, c3), lambda b, t: (0, 0)),
                  pl.BlockSpec((None, tb, LANES), lambda b, t: (b, t, 0)),
                  pl.BlockSpec((None, 32, tb), lambda b, t: (b, 0, t))],
        out_specs=[pl.BlockSpec((tb, width), blk) for _ in range(5)],
        out_shape=[jax.ShapeDtypeStruct((m, width), F32)] +
                  [jax.ShapeDtypeStruct((m, width), BF16) for _ in range(4)],
        scratch_shapes=[pltpu.VMEM((tb + 8, c3), F32)],
        compiler_params=_params(("parallel", "parallel")),
        name="gdn_prep",
    )(raw, raw, conv_state8, conv_w8, col, row)


def _gdn_scan_kernel(u_ref, w_ref, qg_ref, kd_ref, qk_ref, z_ref, row_ref, nw_ref, s0_ref,
                     o_ref, s_ref, *, nh, lc):
    t = pl.program_id(1)
    tb = u_ref.shape[0]
    nw = nw_ref[...]

    @pl.when(t == 0)
    def _():
        s_ref[...] = s0_ref[...]

    heads = [slice(h * HD, (h + 1) * HD) for h in range(nh)]
    states = [s_ref[h] for h in range(nh)]
    for ci in range(tb // lc):
        rows = slice(ci * lc, (ci + 1) * lc)
        wqs = [_dot(jnp.concatenate([w_ref[rows, cols], qg_ref[rows, cols]], axis=0), states[h].astype(BF16))
               for h, cols in enumerate(heads)]
        vbs = [(u_ref[rows, cols] - wqs[h][:lc]).astype(BF16) for h, cols in enumerate(heads)]
        o2s = [_dot(qk_ref[rows, h * HD:h * HD + lc], vbs[h]) for h in range(nh)]
        sds = [_dot_tn(kd_ref[rows, cols], vbs[h]) for h, cols in enumerate(heads)]
        for h, cols in enumerate(heads):
            g_last = row_ref[nh + h:nh + h + 1, ci * lc + lc - 1:ci * lc + lc]
            states[h] = states[h] * jnp.exp(g_last) + sds[h]
            o = wqs[h][lc:] + o2s[h]
            o = o * lax.rsqrt(jnp.mean(o * o, axis=-1, keepdims=True) + RMS_EPS) * nw * _silu(z_ref[rows, cols])
            o_ref[rows, cols] = o.astype(o_ref.dtype)
    for h in range(nh):
        s_ref[h] = states[h]


def gdn_scan(u, w, qg, kd, qk, z, row, norm_w, s0, bsz, t_len, nh, lc):
    m, width = u.shape
    tb = _pick(t_len, (256, 128, 64, 32, 16))
    nt = t_len // tb
    kern = functools.partial(_gdn_scan_kernel, nh=nh, lc=lc)
    blk = lambda b, t: (b * nt + t, 0)
    return pl.pallas_call(
        kern,
        grid=(bsz, nt),
        in_specs=[pl.BlockSpec((tb, width), blk) for _ in range(6)] +
                 [pl.BlockSpec((None, 32, tb), lambda b, t: (b, 0, t)),
                  pl.BlockSpec((1, HD), lambda b, t: (0, 0)),
                  pl.BlockSpec((None, nh, HD, HD), lambda b, t: (b, 0, 0, 0))],
        out_specs=[pl.BlockSpec((tb, width), blk),
                   pl.BlockSpec((None, nh, HD, HD), lambda b, t: (b, 0, 0, 0))],
        out_shape=[jax.ShapeDtypeStruct((m, width), BF16),
                   jax.ShapeDtypeStruct((bsz, nh, HD, HD), F32)],
        compiler_params=_params(("parallel", "arbitrary")),
        name="gdn_scan",
    )(u, w, qg, kd, qk, z, row, norm_w, s0)


def _out_ln_kernel(a_ref, b_ref, w_ref, res_ref, g_ref, beta_ref, o_ref, ob_ref, *, alpha):
    ka = a_ref.shape[1]
    y = _dot(a_ref[...], w_ref[0:ka, :]) + _dot(b_ref[...], w_ref[ka:, :])
    y = _ln_rows(alpha * res_ref[...] + y, g_ref[...], beta_ref[...])
    o_ref[...] = y
    ob_ref[...] = y.astype(BF16)


def out_proj_ln(mix_a, mix_b, w, res, g, beta, alpha):
    m, ka = mix_a.shape
    kb = mix_b.shape[1]
    d = w.shape[1]
    tm = _pick(m, (512, 256, 128, 64, 32, 16, 8))
    kern = functools.partial(_out_ln_kernel, alpha=alpha)
    row = lambda i: (i, 0)
    fixed = lambda i: (0, 0)
    return pl.pallas_call(
        kern,
        grid=(m // tm,),
        in_specs=[pl.BlockSpec((tm, ka), row), pl.BlockSpec((tm, kb), row),
                  pl.BlockSpec((ka + kb, d), fixed, pipeline_mode=pl.Buffered(1)),
                  pl.BlockSpec((tm, d), row),
                  pl.BlockSpec((1, d), fixed), pl.BlockSpec((1, d), fixed)],
        out_specs=[pl.BlockSpec((tm, d), row), pl.BlockSpec((tm, d), row)],
        out_shape=[jax.ShapeDtypeStruct((m, d), F32), jax.ShapeDtypeStruct((m, d), BF16)],
        compiler_params=_params(("parallel",)),
        name="out_proj_ln",
    )(mix_a, mix_b, w, res, g.reshape(1, d), beta.reshape(1, d))


def _ffn_up_kernel(*refs, cast_down):
    if cast_down:
        x_ref, wg_ref, wu_ref, wd_ref, h_ref, wdb_ref, wgb_ref, wub_ref = refs
    else:
        x_ref, wg_ref, wu_ref, h_ref, wgb_ref, wub_ref = refs

    @pl.when(pl.program_id(1) == 0)
    def _():
        for r0 in range(0, wg_ref.shape[0], CAST_ROWS):
            rows = slice(r0, r0 + CAST_ROWS)
            wgb_ref[rows, :] = wg_ref[rows, :].astype(BF16)
            wub_ref[rows, :] = wu_ref[rows, :].astype(BF16)
        if cast_down:
            wdb_ref[...] = wd_ref[...].astype(BF16)

    x = x_ref[...]
    gate = _dot(x, wgb_ref[...])
    up = _dot(x, wub_ref[...])
    h_ref[...] = (_silu(gate) * up).astype(h_ref.dtype)


def ffn_up(xb, wg, wu, layer, wd=None):
    m, d = xb.shape
    f = wg.shape[2]
    assert d % CAST_ROWS == 0
    tm = _pick(m, (1024, 512, 256, 128, 64, 32, 16, 8))
    tf = _pick(f, (512, 256, 128))
    wspec = pl.BlockSpec((None, d, tf), lambda j, i: (layer, 0, j))
    in_specs = [pl.BlockSpec((tm, d), lambda j, i: (i, 0)), wspec, wspec]
    out_specs = [pl.BlockSpec((tm, tf), lambda j, i: (i, j))]
    out_shape = [jax.ShapeDtypeStruct((m, f), BF16)]
    args = [xb, wg, wu]
    if wd is not None:
        dout = wd.shape[2]
        in_specs.append(pl.BlockSpec((None, tf, dout), lambda j, i: (layer, j, 0)))
        out_specs.append(pl.BlockSpec((tf, dout), lambda j, i: (j, 0)))
        out_shape.append(jax.ShapeDtypeStruct((f, dout), BF16))
        args.append(wd)
    return pl.pallas_call(
        functools.partial(_ffn_up_kernel, cast_down=wd is not None),
        grid=(f // tf, m // tm),
        in_specs=in_specs,
        out_specs=out_specs,
        out_shape=out_shape,
        scratch_shapes=[pltpu.VMEM((d, tf), BF16), pltpu.VMEM((d, tf), BF16)],
        compiler_params=_params(("parallel", "arbitrary")),
        name="ffn_up",
    )(*args)


def _ffn_down_kernel(h_ref, w_ref, res_ref, g_ref, beta_ref, o_ref, ob_ref, *, alpha):
    y = _dot(h_ref[...], w_ref[...])
    y = _ln_rows(alpha * res_ref[...] + y, g_ref[...], beta_ref[...])
    o_ref[...] = y
    ob_ref[...] = y.astype(BF16)


def ffn_down_ln(h, w, res, g, beta, alpha):
    m, f = h.shape
    d = w.shape[1]
    tm = _pick(m, (256, 128, 64, 32, 16, 8))
    kern = functools.partial(_ffn_down_kernel, alpha=alpha)
    row = lambda i: (i, 0)
    fixed = lambda i: (0, 0)
    return pl.pallas_call(
        kern,
        grid=(m // tm,),
        in_specs=[pl.BlockSpec((tm, f), row),
                  pl.BlockSpec((f, d), fixed, pipeline_mode=pl.Buffered(1)),
                  pl.BlockSpec((tm, d), row),
                  pl.BlockSpec((1, d), fixed), pl.BlockSpec((1, d), fixed)],
        out_specs=[pl.BlockSpec((tm, d), row), pl.BlockSpec((tm, d), row)],
        out_shape=[jax.ShapeDtypeStruct((m, d), F32), jax.ShapeDtypeStruct((m, d), BF16)],
        compiler_params=_params(("parallel",)),
        name="ffn_down_ln",
    )(h, w, res, g.reshape(1, d), beta.reshape(1, d))


def _layer(x, xb, bsz, t_len, layer, wts, lw, fox_cache, conv_state, s0, kv_prev, alpha):
    nh = lw["nh"]
    width = nh * HD
    lc = min(t_len, CHUNK)
    w_in = wts["w_in"]
    k_prev, v_prev = kv_prev if kv_prev is not None else (None, None)
    qb, = proj_segments(xb, w_in, layer, 0, 1, 0, (BF16,), out_scale=HD ** -0.5)
    kb, k_stack = proj_segments(xb, w_in, layer, 1, 1, 0, (BF16,), prev=k_prev, stack=True)
    vb, v_stack = proj_segments(xb, w_in, layer, 2, 1, 0, (BF16,), prev=v_prev, stack=True)
    raw, = proj_segments(xb, w_in, layer, 3, 3, nh, (F32,))
    z, = proj_segments(xb, w_in, layer, 6, 1, 3 * nh, (F32,))
    logf, col, row, qa, ka = gates(xb, w_in, layer, lw["gate_bias"], lw["alog_row"], bsz, t_len, lc, nh)

    if fox_cache is None:
        mix_f = fox_prompt(qb, qa, kb, ka, vb, lw["fox_norm_w"], bsz, t_len, nh)
    else:
        k_cache, v_cache, lf_cache = fox_cache
        sfx = cache_gate_suffix(jnp.transpose(lf_cache, (0, 2, 1)))
        sfx_flat = jnp.transpose(sfx, (0, 2, 1)).reshape(bsz, 1, -1)
        mix_f = fox_sample(qb, kb, vb, k_cache, v_cache, layer, sfx_flat, col, row,
                           lw["fox_norm_w"], bsz, t_len, nh)

    conv_state8 = jnp.pad(conv_state, ((0, 0), (8 - (CONV_W - 1), 0), (0, 0)))
    u, w, qg, kd, qk = gdn_prep(raw, conv_state8, lw["conv_w8"], col, row, bsz, t_len, nh, lc)
    mix_g, s_fin = gdn_scan(u, w, qg, kd, qk, z, row, lw["gdn_norm_w"], s0, bsz, t_len, nh, lc)

    x1, x1b = out_proj_ln(mix_f, mix_g, lw["w_out"], x, lw["ln_mix_g"], lw["ln_mix_b"], alpha)
    if "w_down" in lw:
        hmid, = ffn_up(x1b, wts["w_gate"], wts["w_up"], layer)
    else:
        hmid, lw["w_down"] = ffn_up(x1b, wts["w_gate"], wts["w_up"], layer, wd=wts["w_down"])
    x2, x2b = ffn_down_ln(hmid, lw["w_down"], x1, lw["ln_ffn_g"], lw["ln_ffn_b"], alpha)

    new_buf = raw.reshape(bsz, t_len, -1)[:, t_len - (CONV_W - 1):, :]
    return x2, x2b, (k_stack, v_stack), (logf, s_fin, new_buf)


def kernel(x_prompt, x_sample, cache_fox_k, cache_fox_v, cache_fox_logf, state_gdn, state_gdn_conv, ln_in_g, ln_in_b, w_in, fox_f_bias, fox_norm_w, gdn_conv_w, gdn_a_log, gdn_dt_bias, gdn_norm_w, w_out, ln_mix_g, ln_mix_b, ffn_w_gate, ffn_w_up, ffn_w_down, ln_ffn_g, ln_ffn_b):
    depth = w_in.shape[0]
    bp, tp, d = x_prompt.shape
    bs, ts, _ = x_sample.shape
    nh = fox_f_bias.shape[1]
    assert fox_norm_w.shape[1] == HD and gdn_norm_w.shape[1] == HD and gdn_a_log.shape[1] == nh
    width = nh * HD
    assert width == SEG and w_in.shape[2] == 7 * width + 3 * nh and gdn_conv_w.shape[2] == 3 * width
    alpha = (2 * depth) ** 0.25

    xp, xpb = layer_norm_rows(x_prompt.reshape(bp * tp, d), ln_in_g, ln_in_b)
    xs, xsb = layer_norm_rows(x_sample.reshape(bs * ts, d), ln_in_g, ln_in_b)

    zero_conv = jnp.zeros((bp, CONV_W - 1, 3 * width), F32)
    zero_state = jnp.zeros((bp, nh, HD, HD), F32)
    k_cache, v_cache = cache_fox_k, cache_fox_v
    wts = dict(w_in=jnp.swapaxes(w_in, 1, 2), w_gate=ffn_w_gate, w_up=ffn_w_up, w_down=ffn_w_down)
    kv_p = kv_s = None
    ps, ss = [], []
    for l in range(depth):
        gate_bias = jnp.concatenate([fox_f_bias[l], gdn_dt_bias[l], jnp.zeros((LANES - 2 * nh,), F32)]).reshape(1, LANES)
        alog_row = jnp.concatenate([jnp.zeros((nh,), F32), gdn_a_log[l], jnp.zeros((LANES - 2 * nh,), F32)]).reshape(1, LANES)
        lw = dict(
            nh=nh,
            gate_bias=gate_bias, alog_row=alog_row,
            fox_norm_w=fox_norm_w[l].reshape(1, HD),
            gdn_norm_w=gdn_norm_w[l].reshape(1, HD),
            conv_w8=jnp.pad(gdn_conv_w[l], ((0, 8 - CONV_W), (0, 0))),
            w_out=cast_bf16(w_out, l),
            ln_mix_g=ln_mix_g[l], ln_mix_b=ln_mix_b[l],
            ln_ffn_g=ln_ffn_g[l], ln_ffn_b=ln_ffn_b[l],
        )
        xp, xpb, kv_p, st_p = _layer(xp, xpb, bp, tp, l, wts, lw, None, zero_conv, zero_state, kv_p, alpha)
        xs, xsb, kv_s, st_s = _layer(xs, xsb, bs, ts, l, wts, lw, (k_cache, v_cache, cache_fox_logf[l]),
                                     state_gdn_conv[l], state_gdn[l], kv_s, alpha)
        ps.append(st_p)
        ss.append(st_s)

    def stacked(states, i):
        return jnp.stack([st[i] for st in states], axis=0)

    def heads(stack, bsz, t_len):
        return stack.reshape(depth, bsz, t_len, nh, HD)

    return (xp.reshape(bp, tp, d), xs.reshape(bs, ts, d),
            heads(kv_p[0], bp, tp), heads(kv_p[1], bp, tp), stacked(ps, 0), stacked(ps, 1), stacked(ps, 2),
            heads(kv_s[0], bs, ts), heads(kv_s[1], bs, ts), stacked(ss, 0), stacked(ss, 1), stacked(ss, 2))
```

```python
import functools
import math

import jax
import jax.numpy as jnp
from jax import lax
from jax.experimental import pallas as pl
from jax.experimental.pallas import tpu as pltpu

LN_EPS = 1e-5
RMS_EPS = 1e-6
L2_EPS = 1e-6
CHUNK = 64
CONV_W = 4
HD = 128
LANES = 128
VMEM_LIMIT = 56 * 1024 * 1024

F32 = jnp.float32
BF16 = jnp.bfloat16


def _pick(dim, cands):
    for c in cands:
        if c <= dim and dim % c == 0:
            return c
    return dim


def _params(sem):
    return pltpu.CompilerParams(dimension_semantics=sem, vmem_limit_bytes=VMEM_LIMIT)


def _split3(x):
    hi = x.astype(BF16)
    r1 = x - hi.astype(F32)
    mid = r1.astype(BF16)
    lo = (r1 - mid.astype(F32)).astype(BF16)
    return hi, mid, lo


def _dot(a, b):
    return jnp.dot(a, b, preferred_element_type=F32)


def _dot_nt(a, b):
    return lax.dot_general(a, b, (((1,), (1,)), ((), ())), preferred_element_type=F32)


def _dot_tn(a, b):
    return lax.dot_general(a, b, (((0,), (0,)), ((), ())), preferred_element_type=F32)


def _ln_rows(y, g, b):
    mu = jnp.mean(y, axis=-1, keepdims=True)
    d = y - mu
    var = jnp.mean(d * d, axis=-1, keepdims=True)
    return d * lax.rsqrt(var + LN_EPS) * g + b


def _silu(x):
    return x / (1.0 + jnp.exp(-x))


def _ln_kernel(x_ref, g_ref, b_ref, o_ref, ob_ref):
    y = _ln_rows(x_ref[...], g_ref[...], b_ref[...])
    o_ref[...] = y
    ob_ref[...] = y.astype(BF16)


def layer_norm_rows(x, g, b):
    m, d = x.shape
    tm = _pick(m, (512, 256, 128, 64, 32, 16, 8))
    return pl.pallas_call(
        _ln_kernel,
        grid=(m // tm,),
        in_specs=[pl.BlockSpec((tm, d), lambda i: (i, 0)),
                  pl.BlockSpec((1, d), lambda i: (0, 0)),
                  pl.BlockSpec((1, d), lambda i: (0, 0))],
        out_specs=[pl.BlockSpec((tm, d), lambda i: (i, 0)),
                   pl.BlockSpec((tm, d), lambda i: (i, 0))],
        out_shape=[jax.ShapeDtypeStruct((m, d), F32), jax.ShapeDtypeStruct((m, d), BF16)],
        compiler_params=_params(("parallel",)),
        name="ln_in",
    )(x, g.reshape(1, d), b.reshape(1, d))


def _cast_kernel(w_ref, o_ref):
    o_ref[...] = w_ref[...].astype(o_ref.dtype)


def cast_bf16(w, layer):
    _, k, n = w.shape
    tk = _pick(k, (512, 256, 128, 64, 32, 16))
    return pl.pallas_call(
        _cast_kernel,
        grid=(k // tk,),
        in_specs=[pl.BlockSpec((None, tk, n), lambda i: (layer, i, 0))],
        out_specs=pl.BlockSpec((tk, n), lambda i: (i, 0)),
        out_shape=jax.ShapeDtypeStruct((k, n), BF16),
        compiler_params=_params(("parallel",)),
        name="cast_bf16",
    )(w)


HEAD_GROUP = 4
SEG = 1024
CAST_ROWS = 256


def _proj_kernel(*refs, shift, n_out, n_prev, stack, out_scale):
    it = iter(refs)
    x_ref = next(it)
    wm_ref = next(it)
    wn_ref = next(it) if shift else None
    prev_ref = next(it) if n_prev else None
    o_refs = [next(it) for _ in range(n_out)]
    stack_ref = next(it) if stack else None
    wb_ref = next(it)

    @pl.when(pl.program_id(1) == 0)
    def _():
        for c0 in range(0, wm_ref.shape[1], CAST_ROWS):
            cols = slice(c0, c0 + CAST_ROWS)
            wm = wm_ref[:, cols]
            if shift:
                wm = jnp.concatenate([wm[shift:, :], wn_ref[0:shift, cols]], axis=0)
            wb_ref[cols, :] = wm.T.astype(BF16)

    acc = _dot(x_ref[...], wb_ref[...])
    for o in o_refs:
        o[...] = (acc if out_scale is None else acc * out_scale).astype(o.dtype)
    if stack:
        for l in range(n_prev):
            stack_ref[l] = prev_ref[l]
        stack_ref[n_prev] = acc.reshape(acc.shape[0], SEG // HD, HD)


def proj_segments(xb, wt, layer, seg0, nseg, shift, out_dtypes, prev=None, stack=False, out_scale=None):
    m, k = xb.shape
    assert k % CAST_ROWS == 0 and not (stack and nseg != 1) and shift % 8 == 0
    nb = 8 if shift <= 8 else 32
    assert shift <= nb and SEG % nb == 0
    n_prev = 0 if prev is None else prev.shape[0]
    tm = _pick(m, (1024, 512, 256, 128, 64, 32, 16, 8))
    single = pl.Buffered(1) if nseg == 1 else None
    in_specs = [pl.BlockSpec((tm, k), lambda j, i: (i, 0)),
                pl.BlockSpec((None, SEG, k), lambda j, i: (layer, seg0 + j, 0), pipeline_mode=single)]
    args = [xb, wt]
    if shift:
        in_specs.append(pl.BlockSpec((None, nb, k), lambda j, i: (layer, (seg0 + j + 1) * (SEG // nb), 0),
                                     pipeline_mode=single))
        args.append(wt)
    if n_prev:
        in_specs.append(pl.BlockSpec((n_prev, tm, SEG // HD, HD), lambda j, i: (0, i, 0, 0)))
        args.append(prev)
    out_specs = [pl.BlockSpec((tm, SEG), lambda j, i: (i, j)) for _ in out_dtypes]
    out_shape = [jax.ShapeDtypeStruct((m, nseg * SEG), dt) for dt in out_dtypes]
    if stack:
        out_specs.append(pl.BlockSpec((n_prev + 1, tm, SEG // HD, HD), lambda j, i: (0, i, 0, 0)))
        out_shape.append(jax.ShapeDtypeStruct((n_prev + 1, m, SEG // HD, HD), F32))
    kern = functools.partial(_proj_kernel, shift=shift, n_out=len(out_dtypes), n_prev=n_prev, stack=stack,
                             out_scale=out_scale)
    return pl.pallas_call(
        kern,
        grid=(nseg, m // tm),
        in_specs=in_specs,
        out_specs=out_specs,
        out_shape=out_shape,
        scratch_shapes=[pltpu.VMEM((k, SEG), BF16)],
        compiler_params=_params(("parallel", "arbitrary")),
        name="proj",
    )(*args)


def _gates_kernel(x_ref, wf_ref, wa_ref, wb_ref, bias_ref, alog_ref, logf_ref, col_ref, row_ref, qa_ref, ka_ref,
                  carry_ref, *, sb, chunk, nh):
    t = pl.program_id(1)
    tb = x_ref.shape[0]

    @pl.when(t == 0)
    def _():
        carry_ref[...] = jnp.zeros_like(carry_ref)

    lane = lax.broadcasted_iota(jnp.int32, (1, LANES), 1)
    wt = jnp.concatenate([wf_ref[...], wa_ref[...], wb_ref[...],
                          jnp.zeros((LANES - 3 * nh, wf_ref.shape[1]), F32)], axis=0).astype(BF16)
    raw = _dot_nt(x_ref[...], wt) + bias_ref[...]
    sp_neg = jnp.maximum(-raw, 0.0) + jnp.log1p(jnp.exp(-jnp.abs(raw)))
    sp_pos = jnp.maximum(raw, 0.0) + jnp.log1p(jnp.exp(-jnp.abs(raw)))
    logf = -sp_neg
    g = -jnp.exp(alog_ref[...]) * sp_pos
    beta = 1.0 / (1.0 + jnp.exp(-raw))
    is_f = lane < nh
    is_g = (lane >= nh) & (lane < 2 * nh)
    is_b = (lane >= 2 * nh) & (lane < 3 * nh)
    logf_ref[...] = logf[:, :nh]

    r = lax.broadcasted_iota(jnp.int32, (sb, sb), 0)
    c = lax.broadcasted_iota(jnp.int32, (sb, sb), 1)
    tri_full = (r >= c).astype(BF16)
    tri_chunk = ((r >= c) & ((r // chunk) == (c // chunk))).astype(BF16)
    er = lax.broadcasted_iota(jnp.int32, (32, LANES), 0)
    ec = lax.broadcasted_iota(jnp.int32, (32, LANES), 1)
    sel = (er == ec).astype(BF16)
    wide = nh * HD
    gr = lax.broadcasted_iota(jnp.int32, (LANES, wide), 0)
    gc = lax.broadcasted_iota(jnp.int32, (LANES, wide), 1)
    gh, gj = gr % nh, gr // nh
    spread_q = ((gc == gh * HD + 3 + gj) & (gj < 3)).astype(BF16)
    spread_k = ((gc == gh * HD + gj) & (gj < 3)).astype(BF16)
    lane_in_head = lax.broadcasted_iota(jnp.int32, (1, wide), 1) % HD
    ones_q = (lane_in_head < 3).astype(F32)
    ones_k = ((lane_in_head >= 3) & (lane_in_head < 6)).astype(F32)

    carry = carry_ref[...]
    for s in range(tb // sb):
        rows = slice(s * sb, (s + 1) * sb)
        lf = jnp.where(is_f, logf[rows], 0.0)
        gg = jnp.where(is_g, g[rows], 0.0)
        cs = jnp.zeros((sb, LANES), F32)
        for part in _split3(lf):
            cs = cs + _dot(tri_full, part)
        for part in _split3(gg):
            cs = cs + _dot(tri_chunk, part)
        cs = cs + carry
        carry = jnp.where(is_f, cs[sb - 1:sb, :], 0.0)
        colv = jnp.where(is_b, beta[rows], cs)
        col_ref[rows, :] = colv
        rv = jnp.zeros((32, sb), F32)
        terms = jnp.zeros((sb, LANES), F32)
        for j, part in enumerate(_split3(colv)):
            rv = rv + _dot_nt(sel, part)
            pf = jnp.where(is_f, part.astype(F32), 0.0)
            terms = terms + (pf if j == 0 else pltpu.roll(pf, j * nh, 1))
        tb16 = terms.astype(BF16)
        row_ref[:, rows] = rv
        qa_ref[rows, :] = (ones_q + _dot(tb16, spread_q)).astype(BF16)
        ka_ref[rows, :] = (ones_k - _dot(tb16, spread_k)).astype(BF16)
    carry_ref[...] = carry


def gates(xb, wt, layer, bias_row, alog_row, bsz, t_len, chunk, nh):
    m, d = xb.shape
    tb = _pick(t_len, (512, 256, 128, 64, 32, 16))
    sb = min(tb, 128)
    nt = t_len // tb
    width = nh * HD
    assert nh == 8 and 3 * nh <= LANES
    f_blk = 3 * width // nh
    a_blk = (6 * width + nh) // nh
    kern = functools.partial(_gates_kernel, sb=sb, chunk=chunk, nh=nh)
    return pl.pallas_call(
        kern,
        grid=(bsz, nt),
        in_specs=[pl.BlockSpec((tb, d), lambda b, t: (b * nt + t, 0)),
                  pl.BlockSpec((None, nh, d), lambda b, t: (layer, f_blk, 0)),
                  pl.BlockSpec((None, nh, d), lambda b, t: (layer, a_blk, 0)),
                  pl.BlockSpec((None, nh, d), lambda b, t: (layer, a_blk + 1, 0)),
                  pl.BlockSpec((1, LANES), lambda b, t: (0, 0)),
                  pl.BlockSpec((1, LANES), lambda b, t: (0, 0))],
        out_specs=[pl.BlockSpec((None, tb, nh), lambda b, t: (b, t, 0)),
                   pl.BlockSpec((None, tb, LANES), lambda b, t: (b, t, 0)),
                   pl.BlockSpec((None, 32, tb), lambda b, t: (b, 0, t)),
                   pl.BlockSpec((tb, width), lambda b, t: (b * nt + t, 0)),
                   pl.BlockSpec((tb, width), lambda b, t: (b * nt + t, 0))],
        out_shape=[jax.ShapeDtypeStruct((bsz, t_len, nh), F32),
                   jax.ShapeDtypeStruct((bsz, t_len, LANES), F32),
                   jax.ShapeDtypeStruct((bsz, 32, t_len), F32),
                   jax.ShapeDtypeStruct((m, width), BF16),
                   jax.ShapeDtypeStruct((m, width), BF16)],
        scratch_shapes=[pltpu.VMEM((1, LANES), F32)],
        compiler_params=_params(("parallel", "arbitrary")),
        name="gates",
    )(xb, wt, wt, wt, bias_row, alog_row)


def _fox_prompt_kernel(q_ref, qa_ref, k_ref, ka_ref, v_ref, nw_ref, o_ref, q2_ref, m_ref, acc_ref, *, nh, tq):
    qi = pl.program_id(1)
    nw = nw_ref[...]
    r = lax.broadcasted_iota(jnp.int32, (tq, LANES), 0)
    c = lax.broadcasted_iota(jnp.int32, (tq, LANES), 1)
    heads = [slice(h * HD, (h + 1) * HD) for h in range(nh)]
    for h, cols in enumerate(heads):
        q2_ref[h] = jnp.concatenate([q_ref[:, cols], qa_ref[:, cols]], axis=-1)

    def scores(start, width, group, masked):
        out = []
        for h in group:
            cols = heads[h]
            k2 = jnp.concatenate([k_ref[pl.ds(start, width), cols], ka_ref[pl.ds(start, width), cols]], axis=-1)
            s = _dot_nt(q2_ref[h], k2)
            pieces = [s[:, k * LANES:(k + 1) * LANES] for k in range(width // LANES)]
            if masked:
                pieces = [jnp.where(r >= c + k * LANES, sk, -jnp.inf) for k, sk in enumerate(pieces)]
            out.append(pieces)
        return out

    def row_max(pieces):
        mx = pieces[0]
        for sk in pieces[1:]:
            mx = jnp.maximum(mx, sk)
        return jnp.broadcast_to(jnp.max(mx, axis=-1, keepdims=True), (tq, LANES))

    def pv(p_pieces, start, width, h):
        p = jnp.concatenate(p_pieces, axis=-1).astype(BF16)
        v_aug = jnp.concatenate([v_ref[pl.ds(start, width), heads[h]], jnp.ones((width, LANES), BF16)], axis=-1)
        return _dot(p, v_aug)

    groups = [list(range(g, min(g + HEAD_GROUP, nh))) for g in range(0, nh, HEAD_GROUP)]

    def block(start, width, first):
        pending = scores(start, width, groups[0], first)
        for gi, group in enumerate(groups):
            ss = pending
            if gi + 1 < len(groups):
                pending = scores(start, width, groups[gi + 1], first)
            ps, alphas = [], []
            for h, pieces in zip(group, ss):
                mx = row_max(pieces)
                if first:
                    m_new = mx
                else:
                    m = m_ref[h]
                    m_new = jnp.maximum(m, mx)
                    alphas.append(jnp.exp(m - m_new))
                m_ref[h] = m_new
                ps.append([jnp.exp(sk - m_new) for sk in pieces])
            for i, (h, pp) in enumerate(zip(group, ps)):
                if first:
                    acc_ref[h] = pv(pp, start, width, h)
                else:
                    a = alphas[i]
                    acc_ref[h] = jnp.concatenate([a, a], axis=-1) * acc_ref[h] + pv(pp, start, width, h)

    block(pl.multiple_of(qi * tq, tq), tq, True)

    @pl.when(qi % 2 == 1)
    def _():
        block(pl.multiple_of((qi - 1) * tq, tq), tq, False)

    def step(j, carry):
        block(pl.multiple_of(j * (2 * tq), 2 * tq), 2 * tq, False)
        return carry

    lax.fori_loop(0, qi // 2, step, 0)
    for h in range(nh):
        acc = acc_ref[h]
        o = acc[:, :HD] / acc[:, HD:]
        o = o * lax.rsqrt(jnp.mean(o * o, axis=-1, keepdims=True) + RMS_EPS) * nw
        o_ref[:, heads[h]] = o.astype(o_ref.dtype)


def fox_prompt(qb, qa, kb, ka, vb, norm_w, bsz, t_len, nh):
    tq = _pick(t_len, (256, 128))
    assert tq % LANES == 0 and t_len % tq == 0
    nq = t_len // tq
    w = nh * HD
    kern = functools.partial(_fox_prompt_kernel, nh=nh, tq=tq)
    qspec = pl.BlockSpec((tq, w), lambda b, i: (b * nq + i, 0))
    kspec = pl.BlockSpec((t_len, w), lambda b, i: (b, 0))
    return pl.pallas_call(
        kern,
        grid=(bsz, nq),
        in_specs=[qspec, qspec, kspec, kspec, kspec, pl.BlockSpec((1, HD), lambda b, i: (0, 0))],
        out_specs=pl.BlockSpec((tq, w), lambda b, i: (b * nq + i, 0)),
        out_shape=jax.ShapeDtypeStruct((bsz * t_len, w), BF16),
        scratch_shapes=[pltpu.VMEM((nh, tq, 2 * HD), BF16), pltpu.VMEM((nh, tq, LANES), F32),
                        pltpu.VMEM((nh, tq, 2 * HD), F32)],
        compiler_params=_params(("parallel", "parallel")),
        name="fox_prompt",
    )(qb, qa, kb, ka, vb, norm_w)


def _cache_gate_kernel(lf_ref, o_ref):
    past = lf_ref.shape[1]
    r = lax.broadcasted_iota(jnp.int32, (past, past), 0)
    c = lax.broadcasted_iota(jnp.int32, (past, past), 1)
    later = (r > c).astype(BF16)
    acc = jnp.zeros(lf_ref.shape, F32)
    for part in _split3(lf_ref[...]):
        acc = acc + _dot(part, later)
    o_ref[...] = acc


def cache_gate_suffix(lf_t):
    bsz, nh, past = lf_t.shape
    rows = bsz * nh
    return pl.pallas_call(
        _cache_gate_kernel,
        grid=(1,),
        in_specs=[pl.BlockSpec((rows, past), lambda i: (0, 0))],
        out_specs=pl.BlockSpec((rows, past), lambda i: (0, 0)),
        out_shape=jax.ShapeDtypeStruct((rows, past), F32),
        compiler_params=_params(("arbitrary",)),
        name="cache_gate_suffix",
    )(lf_t.reshape(rows, past)).reshape(bsz, nh, past)


def _fox_sample_kernel(q_ref, kn_ref, vn_ref, kc_ref, vc_ref, sfx_ref, col_ref, row_ref, nw_ref, o_ref, *, nh):
    t_len = q_ref.shape[0]
    past = kc_ref.shape[0]
    nw = nw_ref[...]
    heads = [slice(h * HD, (h + 1) * HD) for h in range(nh)]
    q_all = jnp.concatenate([q_ref[:, cols] for cols in heads], axis=0)
    cn_all = jnp.concatenate([col_ref[:, h:h + 1] for h in range(nh)], axis=0)
    k_flat = kc_ref[...].reshape(past * nh, HD).astype(BF16)
    v_flat = vc_ref[...].reshape(past * nh, HD).astype(BF16)
    rr = lax.broadcasted_iota(jnp.int32, (nh * t_len, past * nh), 0)
    cc = lax.broadcasted_iota(jnp.int32, (nh * t_len, past * nh), 1)
    own_head = (cc % nh) == (rr // t_len)
    s_c = jnp.where(own_head, _dot_nt(q_all, k_flat) + (cn_all + sfx_ref[...]), -jnp.inf)
    tr = lax.broadcasted_iota(jnp.int32, (t_len, t_len), 0)
    tc = lax.broadcasted_iota(jnp.int32, (t_len, t_len), 1)
    causal = tr >= tc
    s_n = []
    for h, cols in enumerate(heads):
        sn = _dot_nt(q_ref[:, cols], kn_ref[:, cols]) + (col_ref[:, h:h + 1] - row_ref[h:h + 1, :])
        s_n.append(jnp.where(causal, sn, -jnp.inf))
    m_n = jnp.concatenate([jnp.max(sn, axis=-1, keepdims=True) for sn in s_n], axis=0)
    m = jnp.maximum(jnp.max(s_c, axis=-1, keepdims=True), m_n)
    p_c = jnp.exp(s_c - m)
    l_c = jnp.sum(p_c, axis=-1, keepdims=True)
    o_c = _dot(p_c.astype(BF16), v_flat)
    for h, cols in enumerate(heads):
        rows = slice(h * t_len, (h + 1) * t_len)
        p_n = jnp.exp(s_n[h] - m[rows])
        l = l_c[rows] + jnp.sum(p_n, axis=-1, keepdims=True)
        o = (o_c[rows] + _dot(p_n.astype(BF16), vn_ref[:, cols])) / l
        o = o * lax.rsqrt(jnp.mean(o * o, axis=-1, keepdims=True) + RMS_EPS) * nw
        o_ref[:, cols] = o.astype(o_ref.dtype)


def fox_sample(qb, knb, vnb, k_cache, v_cache, layer, suffix_flat, col, row, norm_w, bsz, t_len, nh):
    w = nh * HD
    past = k_cache.shape[2]
    kern = functools.partial(_fox_sample_kernel, nh=nh)
    return pl.pallas_call(
        kern,
        grid=(bsz,),
        in_specs=[pl.BlockSpec((t_len, w), lambda b: (b, 0)),
                  pl.BlockSpec((t_len, w), lambda b: (b, 0)),
                  pl.BlockSpec((t_len, w), lambda b: (b, 0)),
                  pl.BlockSpec((None, None, past, nh, HD), lambda b: (layer, b, 0, 0, 0)),
                  pl.BlockSpec((None, None, past, nh, HD), lambda b: (layer, b, 0, 0, 0)),
                  pl.BlockSpec((None, 1, past * nh), lambda b: (b, 0, 0)),
                  pl.BlockSpec((None, t_len, LANES), lambda b: (b, 0, 0)),
                  pl.BlockSpec((None, 32, t_len), lambda b: (b, 0, 0)),
                  pl.BlockSpec((1, HD), lambda b: (0, 0))],
        out_specs=pl.BlockSpec((t_len, w), lambda b: (b, 0)),
        out_shape=jax.ShapeDtypeStruct((bsz * t_len, w), BF16),
        compiler_params=_params(("parallel",)),
        name="fox_sample",
    )(qb, knb, vnb, k_cache, v_cache, suffix_flat, col, row, norm_w)


def _gdn_prep_kernel(raw_ref, prev_ref, cbuf_ref, cw_ref, col_ref, row_ref,
                     u_ref, w_ref, qg_ref, kd_ref, qk_ref, ext_ref, *, nh, lc):
    t = pl.program_id(1)
    tb = raw_ref.shape[0]
    width = nh * HD
    pad = 8
    @pl.when(t == 0)
    def _():
        ext_ref[0:pad, :] = cbuf_ref[...]

    @pl.when(t > 0)
    def _():
        ext_ref[0:pad, :] = prev_ref[...]

    ext_ref[pad:pad + tb, :] = raw_ref[...]

    r = lax.broadcasted_iota(jnp.int32, (lc, lc), 0)
    c = lax.broadcasted_iota(jnp.int32, (lc, lc), 1)
    tril = r >= c
    strict = r > c
    eye = (r == c).astype(F32)
    n_sq = int(math.log2(lc)) - 1
    qscale = HD ** -0.5
    zpad = jnp.zeros((lc, HD - lc), BF16) if lc < HD else None

    def conv_act(seg, h):
        lo = seg * width + h * HD
        acc = None
        for i in reversed(range(CONV_W)):
            term = ext_ref[pl.ds(pad - (CONV_W - 1) + i, tb), lo:lo + HD] * cw_ref[i:i + 1, lo:lo + HD]
            acc = term if acc is None else acc + term
        return _silu(acc)

    def l2n(x):
        return x * lax.rsqrt(jnp.sum(x * x, axis=-1, keepdims=True) + L2_EPS)

    units = []
    for h in range(nh):
        cols = slice(h * HD, (h + 1) * HD)
        q_all = l2n(conv_act(0, h)) * qscale
        k_all = l2n(conv_act(1, h))
        v_all = conv_act(2, h)
        for ci in range(tb // lc):
            rows = slice(ci * lc, (ci + 1) * lc)
            q = q_all[rows]
            k = k_all[rows]
            g_col = col_ref[rows, nh + h:nh + h + 1]
            b_col = col_ref[rows, 2 * nh + h:2 * nh + h + 1]
            g_row = row_ref[nh + h:nh + h + 1, rows]
            g_last = g_row[:, lc - 1:lc]
            kb = k * b_col
            qg_ref[rows, cols] = (q * jnp.exp(g_col)).astype(BF16)
            kd_ref[rows, cols] = (k * jnp.exp(g_last - g_col)).astype(BF16)
            units.append(dict(
                rows=rows, cols=cols,
                kbq=jnp.concatenate([kb, q], axis=0).astype(BF16),
                kbf=k.astype(BF16),
                decay=jnp.exp(jnp.where(tril, g_col - g_row, -jnp.inf)),
                rhs=jnp.concatenate([v_all[rows] * b_col, kb * jnp.exp(g_col)], axis=-1).astype(BF16)))

    for un in units:
        akq = _dot_nt(un["kbq"], un["kbf"])
        qkb = (akq[lc:] * un["decay"]).astype(BF16)
        if zpad is not None:
            qkb = jnp.concatenate([qkb, zpad], axis=-1)
        qk_ref[un["rows"], un["cols"]] = qkb
        un["p"] = jnp.where(strict, -(akq[:lc] * un["decay"]), 0.0)
        un["t"] = eye + un["p"]
    for un in units:
        pb = un["p"].astype(BF16)
        un["p"] = _dot(pb, pb)
    for _ in range(n_sq - 1):
        for un in units:
            st = _dot(jnp.concatenate([un["p"], un["t"]], axis=0).astype(BF16), un["p"].astype(BF16))
            un["p"] = st[:lc]
            un["t"] = un["t"] + st[lc:]
    for un in units:
        un["t"] = un["t"] + _dot(un["t"].astype(BF16), un["p"].astype(BF16))
    for un in units:
        uw = _dot(un["t"].astype(BF16), un["rhs"])
        u_ref[un["rows"], un["cols"]] = uw[:, :HD]
        w_ref[un["rows"], un["cols"]] = uw[:, HD:].astype(BF16)


def gdn_prep(raw, conv_state8, conv_w8, col, row, bsz, t_len, nh, lc):
    m = raw.shape[0]
    width = nh * HD
    c3 = 3 * width
    tb = _pick(t_len, (128, 64, 32, 16))
    nt = t_len // tb
    kern = functools.partial(_gdn_prep_kernel, nh=nh, lc=lc)
    blk = lambda b, t: (b * nt + t, 0)
    return pl.pallas_call(
        kern,
        grid=(bsz, nt),
        in_specs=[pl.BlockSpec((tb, c3), blk),
                  pl.BlockSpec((8, c3), lambda b, t: (jnp.maximum((b * nt + t) * (tb // 8) - 1, 0), 0)),
                  pl.BlockSpec((None, 8, c3), lambda b, t: (b, 0, 0)),
                  pl.BlockSpec((8, c3), lambda b, t: (0, 0)),
                  pl.BlockSpec((None, tb, LANES), lambda b, t: (b, t, 0)),
                  pl.BlockSpec((None, 32, tb), lambda b, t: (b, 0, t))],
        out_specs=[pl.BlockSpec((tb, width), blk) for _ in range(5)],
        out_shape=[jax.ShapeDtypeStruct((m, width), F32)] +
                  [jax.ShapeDtypeStruct((m, width), BF16) for _ in range(4)],
        scratch_shapes=[pltpu.VMEM((tb + 8, c3), F32)],
        compiler_params=_params(("parallel", "parallel")),
        name="gdn_prep",
    )(raw, raw, conv_state8, conv_w8, col, row)


def _gdn_scan_kernel(u_ref, w_ref, qg_ref, kd_ref, qk_ref, z_ref, row_ref, nw_ref, s0_ref,
                     o_ref, s_ref, *, nh, lc):
    t = pl.program_id(1)
    nb, tb = u_ref.shape[0], u_ref.shape[1]
    nw = nw_ref[...]

    @pl.when(t == 0)
    def _():
        s_ref[...] = s0_ref[...]

    units = [(p, h, slice(h * HD, (h + 1) * HD)) for p in range(nb) for h in range(nh)]
    states = [s_ref[p, h] for p, h, _ in units]
    for ci in range(tb // lc):
        rows = slice(ci * lc, (ci + 1) * lc)
        wqs = [_dot(jnp.concatenate([w_ref[p, rows, cols], qg_ref[p, rows, cols]], axis=0), states[n].astype(BF16))
               for n, (p, h, cols) in enumerate(units)]
        vbs = [(u_ref[p, rows, cols] - wqs[n][:lc]).astype(BF16) for n, (p, h, cols) in enumerate(units)]
        o2s = [_dot(qk_ref[p, rows, h * HD:h * HD + lc], vbs[n]) for n, (p, h, cols) in enumerate(units)]
        sds = [_dot_tn(kd_ref[p, rows, cols], vbs[n]) for n, (p, h, cols) in enumerate(units)]
        for n, (p, h, cols) in enumerate(units):
            g_last = row_ref[p, nh + h:nh + h + 1, ci * lc + lc - 1:ci * lc + lc]
            states[n] = states[n] * jnp.exp(g_last) + sds[n]
            o = wqs[n][lc:] + o2s[n]
            o = o * lax.rsqrt(jnp.mean(o * o, axis=-1, keepdims=True) + RMS_EPS) * nw * _silu(z_ref[p, rows, cols])
            o_ref[p, rows, cols] = o.astype(o_ref.dtype)
    for n, (p, h, _) in enumerate(units):
        s_ref[p, h] = states[n]


def gdn_scan(u, w, qg, kd, qk, z, row, norm_w, s0, bsz, t_len, nh, lc):
    m, width = u.shape
    tb = _pick(t_len, (256, 128, 64, 32, 16))
    nt = t_len // tb
    nb = 4 if bsz % 4 == 0 else (2 if bsz % 2 == 0 else 1)
    kern = functools.partial(_gdn_scan_kernel, nh=nh, lc=lc)
    rows3 = lambda a: a.reshape(bsz, t_len, width)
    blk = pl.BlockSpec((nb, tb, width), lambda b, t: (b, t, 0))
    mix, s_fin = pl.pallas_call(
        kern,
        grid=(bsz // nb, nt),
        in_specs=[blk for _ in range(6)] +
                 [pl.BlockSpec((nb, 32, tb), lambda b, t: (b, 0, t)),
                  pl.BlockSpec((1, HD), lambda b, t: (0, 0)),
                  pl.BlockSpec((nb, nh, HD, HD), lambda b, t: (b, 0, 0, 0))],
        out_specs=[blk, pl.BlockSpec((nb, nh, HD, HD), lambda b, t: (b, 0, 0, 0))],
        out_shape=[jax.ShapeDtypeStruct((bsz, t_len, width), BF16),
                   jax.ShapeDtypeStruct((bsz, nh, HD, HD), F32)],
        compiler_params=_params(("parallel", "arbitrary")),
        name="gdn_scan",
    )(rows3(u), rows3(w), rows3(qg), rows3(kd), rows3(qk), rows3(z), row, norm_w, s0)
    return mix.reshape(m, width), s_fin


def _out_ln_kernel(a_ref, b_ref, w_ref, res_ref, g_ref, beta_ref, o_ref, ob_ref, *, alpha):
    ka = a_ref.shape[1]
    y = _dot(a_ref[...], w_ref[0:ka, :]) + _dot(b_ref[...], w_ref[ka:, :])
    y = _ln_rows(alpha * res_ref[...] + y, g_ref[...], beta_ref[...])
    o_ref[...] = y
    ob_ref[...] = y.astype(BF16)


def out_proj_ln(mix_a, mix_b, w, res, g, beta, alpha):
    m, ka = mix_a.shape
    kb = mix_b.shape[1]
    d = w.shape[1]
    tm = _pick(m, (512, 256, 128, 64, 32, 16, 8))
    kern = functools.partial(_out_ln_kernel, alpha=alpha)
    row = lambda i: (i, 0)
    fixed = lambda i: (0, 0)
    return pl.pallas_call(
        kern,
        grid=(m // tm,),
        in_specs=[pl.BlockSpec((tm, ka), row), pl.BlockSpec((tm, kb), row),
                  pl.BlockSpec((ka + kb, d), fixed, pipeline_mode=pl.Buffered(1)),
                  pl.BlockSpec((tm, d), row),
                  pl.BlockSpec((1, d), fixed), pl.BlockSpec((1, d), fixed)],
        out_specs=[pl.BlockSpec((tm, d), row), pl.BlockSpec((tm, d), row)],
        out_shape=[jax.ShapeDtypeStruct((m, d), F32), jax.ShapeDtypeStruct((m, d), BF16)],
        compiler_params=_params(("parallel",)),
        name="out_proj_ln",
    )(mix_a, mix_b, w, res, g.reshape(1, d), beta.reshape(1, d))


def _ffn_up_kernel(*refs, cast_down):
    if cast_down:
        x_ref, wg_ref, wu_ref, wd_ref, h_ref, wdb_ref, wgb_ref, wub_ref = refs
    else:
        x_ref, wg_ref, wu_ref, h_ref, wgb_ref, wub_ref = refs

    @pl.when(pl.program_id(1) == 0)
    def _():
        for r0 in range(0, wg_ref.shape[0], CAST_ROWS):
            rows = slice(r0, r0 + CAST_ROWS)
            wgb_ref[rows, :] = wg_ref[rows, :].astype(BF16)
            wub_ref[rows, :] = wu_ref[rows, :].astype(BF16)
        if cast_down:
            wdb_ref[...] = wd_ref[...].astype(BF16)

    x = x_ref[...]
    gate = _dot(x, wgb_ref[...])
    up = _dot(x, wub_ref[...])
    h_ref[...] = (_silu(gate) * up).astype(h_ref.dtype)


def ffn_up(xb, wg, wu, layer, wd=None):
    m, d = xb.shape
    f = wg.shape[2]
    assert d % CAST_ROWS == 0
    tm = _pick(m, (1024, 512, 256, 128, 64, 32, 16, 8))
    tf = _pick(f, (512, 256, 128))
    wspec = pl.BlockSpec((None, d, tf), lambda j, i: (layer, 0, j))
    in_specs = [pl.BlockSpec((tm, d), lambda j, i: (i, 0)), wspec, wspec]
    out_specs = [pl.BlockSpec((tm, tf), lambda j, i: (i, j))]
    out_shape = [jax.ShapeDtypeStruct((m, f), BF16)]
    args = [xb, wg, wu]
    if wd is not None:
        dout = wd.shape[2]
        in_specs.append(pl.BlockSpec((None, tf, dout), lambda j, i: (layer, j, 0)))
        out_specs.append(pl.BlockSpec((tf, dout), lambda j, i: (j, 0)))
        out_shape.append(jax.ShapeDtypeStruct((f, dout), BF16))
        args.append(wd)
    return pl.pallas_call(
        functools.partial(_ffn_up_kernel, cast_down=wd is not None),
        grid=(f // tf, m // tm),
        in_specs=in_specs,
        out_specs=out_specs,
        out_shape=out_shape,
        scratch_shapes=[pltpu.VMEM((d, tf), BF16), pltpu.VMEM((d, tf), BF16)],
        compiler_params=_params(("parallel", "arbitrary")),
        name="ffn_up",
    )(*args)


def _ffn_down_kernel(h_ref, w_ref, res_ref, g_ref, beta_ref, o_ref, ob_ref, *, alpha):
    y = _dot(h_ref[...], w_ref[...])
    y = _ln_rows(alpha * res_ref[...] + y, g_ref[...], beta_ref[...])
    o_ref[...] = y
    ob_ref[...] = y.astype(BF16)


def ffn_down_ln(h, w, res, g, beta, alpha):
    m, f = h.shape
    d = w.shape[1]
    tm = _pick(m, (256, 128, 64, 32, 16, 8))
    kern = functools.partial(_ffn_down_kernel, alpha=alpha)
    row = lambda i: (i, 0)
    fixed = lambda i: (0, 0)
    return pl.pallas_call(
        kern,
        grid=(m // tm,),
        in_specs=[pl.BlockSpec((tm, f), row),
                  pl.BlockSpec((f, d), fixed, pipeline_mode=pl.Buffered(1)),
                  pl.BlockSpec((tm, d), row),
                  pl.BlockSpec((1, d), fixed), pl.BlockSpec((1, d), fixed)],
        out_specs=[pl.BlockSpec((tm, d), row), pl.BlockSpec((tm, d), row)],
        out_shape=[jax.ShapeDtypeStruct((m, d), F32), jax.ShapeDtypeStruct((m, d), BF16)],
        compiler_params=_params(("parallel",)),
        name="ffn_down_ln",
    )(h, w, res, g.reshape(1, d), beta.reshape(1, d))


def _layer(x, xb, bsz, t_len, layer, wts, lw, fox_cache, conv_state, s0, kv_prev, alpha):
    nh = lw["nh"]
    width = nh * HD
    lc = min(t_len, CHUNK)
    w_in = wts["w_in"]
    k_prev, v_prev = kv_prev if kv_prev is not None else (None, None)
    qb, = proj_segments(xb, w_in, layer, 0, 1, 0, (BF16,), out_scale=HD ** -0.5)
    kb, k_stack = proj_segments(xb, w_in, layer, 1, 1, 0, (BF16,), prev=k_prev, stack=True)
    vb, v_stack = proj_segments(xb, w_in, layer, 2, 1, 0, (BF16,), prev=v_prev, stack=True)
    raw, = proj_segments(xb, w_in, layer, 3, 3, nh, (F32,))
    z, = proj_segments(xb, w_in, layer, 6, 1, 3 * nh, (F32,))
    logf, col, row, qa, ka = gates(xb, w_in, layer, lw["gate_bias"], lw["alog_row"], bsz, t_len, lc, nh)

    if fox_cache is None:
        mix_f = fox_prompt(qb, qa, kb, ka, vb, lw["fox_norm_w"], bsz, t_len, nh)
    else:
        k_cache, v_cache, lf_cache = fox_cache
        sfx = cache_gate_suffix(jnp.transpose(lf_cache, (0, 2, 1)))
        sfx_flat = jnp.transpose(sfx, (0, 2, 1)).reshape(bsz, 1, -1)
        mix_f = fox_sample(qb, kb, vb, k_cache, v_cache, layer, sfx_flat, col, row,
                           lw["fox_norm_w"], bsz, t_len, nh)

    conv_state8 = jnp.pad(conv_state, ((0, 0), (8 - (CONV_W - 1), 0), (0, 0)))
    u, w, qg, kd, qk = gdn_prep(raw, conv_state8, lw["conv_w8"], col, row, bsz, t_len, nh, lc)
    mix_g, s_fin = gdn_scan(u, w, qg, kd, qk, z, row, lw["gdn_norm_w"], s0, bsz, t_len, nh, lc)

    x1, x1b = out_proj_ln(mix_f, mix_g, lw["w_out"], x, lw["ln_mix_g"], lw["ln_mix_b"], alpha)
    if "w_down" in lw:
        hmid, = ffn_up(x1b, wts["w_gate"], wts["w_up"], layer)
    else:
        hmid, lw["w_down"] = ffn_up(x1b, wts["w_gate"], wts["w_up"], layer, wd=wts["w_down"])
    x2, x2b = ffn_down_ln(hmid, lw["w_down"], x1, lw["ln_ffn_g"], lw["ln_ffn_b"], alpha)

    new_buf = raw.reshape(bsz, t_len, -1)[:, t_len - (CONV_W - 1):, :]
    return x2, x2b, (k_stack, v_stack), (logf, s_fin, new_buf)


def kernel(x_prompt, x_sample, cache_fox_k, cache_fox_v, cache_fox_logf, state_gdn, state_gdn_conv, ln_in_g, ln_in_b, w_in, fox_f_bias, fox_norm_w, gdn_conv_w, gdn_a_log, gdn_dt_bias, gdn_norm_w, w_out, ln_mix_g, ln_mix_b, ffn_w_gate, ffn_w_up, ffn_w_down, ln_ffn_g, ln_ffn_b):
    depth = w_in.shape[0]
    bp, tp, d = x_prompt.shape
    bs, ts, _ = x_sample.shape
    nh = fox_f_bias.shape[1]
    assert fox_norm_w.shape[1] == HD and gdn_norm_w.shape[1] == HD and gdn_a_log.shape[1] == nh
    width = nh * HD
    assert width == SEG and w_in.shape[2] == 7 * width + 3 * nh and gdn_conv_w.shape[2] == 3 * width
    alpha = (2 * depth) ** 0.25

    xp, xpb = layer_norm_rows(x_prompt.reshape(bp * tp, d), ln_in_g, ln_in_b)
    xs, xsb = layer_norm_rows(x_sample.reshape(bs * ts, d), ln_in_g, ln_in_b)

    zero_conv = jnp.zeros((bp, CONV_W - 1, 3 * width), F32)
    zero_state = jnp.zeros((bp, nh, HD, HD), F32)
    k_cache, v_cache = cache_fox_k, cache_fox_v
    wts = dict(w_in=jnp.swapaxes(w_in, 1, 2), w_gate=ffn_w_gate, w_up=ffn_w_up, w_down=ffn_w_down)
    kv_p = kv_s = None
    ps, ss = [], []
    for l in range(depth):
        gate_bias = jnp.concatenate([fox_f_bias[l], gdn_dt_bias[l], jnp.zeros((LANES - 2 * nh,), F32)]).reshape(1, LANES)
        alog_row = jnp.concatenate([jnp.zeros((nh,), F32), gdn_a_log[l], jnp.zeros((LANES - 2 * nh,), F32)]).reshape(1, LANES)
        lw = dict(
            nh=nh,
            gate_bias=gate_bias, alog_row=alog_row,
            fox_norm_w=fox_norm_w[l].reshape(1, HD),
            gdn_norm_w=gdn_norm_w[l].reshape(1, HD),
            conv_w8=jnp.pad(gdn_conv_w[l], ((0, 8 - CONV_W), (0, 0))),
            w_out=cast_bf16(w_out, l),
            ln_mix_g=ln_mix_g[l], ln_mix_b=ln_mix_b[l],
            ln_ffn_g=ln_ffn_g[l], ln_ffn_b=ln_ffn_b[l],
        )
        xp, xpb, kv_p, st_p = _layer(xp, xpb, bp, tp, l, wts, lw, None, zero_conv, zero_state, kv_p, alpha)
        xs, xsb, kv_s, st_s = _layer(xs, xsb, bs, ts, l, wts, lw, (k_cache, v_cache, cache_fox_logf[l]),
                                     state_gdn_conv[l], state_gdn[l], kv_s, alpha)
        ps.append(st_p)
        ss.append(st_s)

    def stacked(states, i):
        return jnp.stack([st[i] for st in states], axis=0)

    def heads(stack, bsz, t_len):
        return stack.reshape(depth, bsz, t_len, nh, HD)

    return (xp.reshape(bp, tp, d), xs.reshape(bs, ts, d),
            heads(kv_p[0], bp, tp), heads(kv_p[1], bp, tp), stacked(ps, 0), stacked(ps, 1), stacked(ps, 2),
            heads(kv_s[0], bs, ts), heads(kv_s[1], bs, ts), stacked(ss, 0), stacked(ss, 1), stacked(ss, 2))
```

```python
import functools
import math

import jax
import jax.numpy as jnp
from jax import lax
from jax.experimental import pallas as pl
from jax.experimental.pallas import tpu as pltpu

LN_EPS = 1e-5
RMS_EPS = 1e-6
L2_EPS = 1e-6
CHUNK = 64
CONV_W = 4
HD = 128
LANES = 128
VMEM_LIMIT = 56 * 1024 * 1024

F32 = jnp.float32
BF16 = jnp.bfloat16


def _pick(dim, cands):
    for c in cands:
        if c <= dim and dim % c == 0:
            return c
    return dim


def _params(sem):
    return pltpu.CompilerParams(dimension_semantics=sem, vmem_limit_bytes=VMEM_LIMIT)


def _split3(x):
    hi = x.astype(BF16)
    r1 = x - hi.astype(F32)
    mid = r1.astype(BF16)
    lo = (r1 - mid.astype(F32)).astype(BF16)
    return hi, mid, lo


def _dot(a, b):
    return jnp.dot(a, b, preferred_element_type=F32)


def _dot_nt(a, b):
    return lax.dot_general(a, b, (((1,), (1,)), ((), ())), preferred_element_type=F32)


def _dot_tn(a, b):
    return lax.dot_general(a, b, (((0,), (0,)), ((), ())), preferred_element_type=F32)


def _ln_rows(y, g, b):
    mu = jnp.mean(y, axis=-1, keepdims=True)
    d = y - mu
    var = jnp.mean(d * d, axis=-1, keepdims=True)
    return d * lax.rsqrt(var + LN_EPS) * g + b


def _silu(x):
    return x / (1.0 + jnp.exp(-x))


def _ln_kernel(x_ref, g_ref, b_ref, o_ref, ob_ref):
    y = _ln_rows(x_ref[...], g_ref[...], b_ref[...])
    o_ref[...] = y
    ob_ref[...] = y.astype(BF16)


def layer_norm_rows(x, g, b):
    m, d = x.shape
    tm = _pick(m, (512, 256, 128, 64, 32, 16, 8))
    return pl.pallas_call(
        _ln_kernel,
        grid=(m // tm,),
        in_specs=[pl.BlockSpec((tm, d), lambda i: (i, 0)),
                  pl.BlockSpec((1, d), lambda i: (0, 0)),
                  pl.BlockSpec((1, d), lambda i: (0, 0))],
        out_specs=[pl.BlockSpec((tm, d), lambda i: (i, 0)),
                   pl.BlockSpec((tm, d), lambda i: (i, 0))],
        out_shape=[jax.ShapeDtypeStruct((m, d), F32), jax.ShapeDtypeStruct((m, d), BF16)],
        compiler_params=_params(("parallel",)),
        name="ln_in",
    )(x, g.reshape(1, d), b.reshape(1, d))


def _cast_kernel(w_ref, o_ref):
    o_ref[...] = w_ref[...].astype(o_ref.dtype)


def cast_bf16(w, layer):
    _, k, n = w.shape
    tk = _pick(k, (512, 256, 128, 64, 32, 16))
    return pl.pallas_call(
        _cast_kernel,
        grid=(k // tk,),
        in_specs=[pl.BlockSpec((None, tk, n), lambda i: (layer, i, 0))],
        out_specs=pl.BlockSpec((tk, n), lambda i: (i, 0)),
        out_shape=jax.ShapeDtypeStruct((k, n), BF16),
        compiler_params=_params(("parallel",)),
        name="cast_bf16",
    )(w)


HEAD_GROUP = 1
SEG = 1024
CAST_ROWS = 256


def _proj_kernel(*refs, shift, n_out, n_prev, stack, out_scale):
    it = iter(refs)
    x_ref = next(it)
    wm_ref = next(it)
    wn_ref = next(it) if shift else None
    prev_ref = next(it) if n_prev else None
    o_refs = [next(it) for _ in range(n_out)]
    stack_ref = next(it) if stack else None
    wb_ref = next(it)

    @pl.when(pl.program_id(1) == 0)
    def _():
        for c0 in range(0, wm_ref.shape[1], CAST_ROWS):
            cols = slice(c0, c0 + CAST_ROWS)
            wm = wm_ref[:, cols]
            if shift:
                wm = jnp.concatenate([wm[shift:, :], wn_ref[0:shift, cols]], axis=0)
            wb_ref[cols, :] = wm.T.astype(BF16)

    acc = _dot(x_ref[...], wb_ref[...])
    for o in o_refs:
        o[...] = (acc if out_scale is None else acc * out_scale).astype(o.dtype)
    if stack:
        for l in range(n_prev):
            stack_ref[l] = prev_ref[l]
        stack_ref[n_prev] = acc.reshape(acc.shape[0], SEG // HD, HD)


def proj_segments(xb, wt, layer, seg0, nseg, shift, out_dtypes, prev=None, stack=False, out_scale=None):
    m, k = xb.shape
    assert k % CAST_ROWS == 0 and not (stack and nseg != 1) and shift % 8 == 0
    nb = 8 if shift <= 8 else 32
    assert shift <= nb and SEG % nb == 0
    n_prev = 0 if prev is None else prev.shape[0]
    tm = _pick(m, (1024, 512, 256, 128, 64, 32, 16, 8))
    single = pl.Buffered(1) if nseg == 1 else None
    in_specs = [pl.BlockSpec((tm, k), lambda j, i: (i, 0)),
                pl.BlockSpec((None, SEG, k), lambda j, i: (layer, seg0 + j, 0), pipeline_mode=single)]
    args = [xb, wt]
    if shift:
        in_specs.append(pl.BlockSpec((None, nb, k), lambda j, i: (layer, (seg0 + j + 1) * (SEG // nb), 0),
                                     pipeline_mode=single))
        args.append(wt)
    if n_prev:
        in_specs.append(pl.BlockSpec((n_prev, tm, SEG // HD, HD), lambda j, i: (0, i, 0, 0)))
        args.append(prev)
    out_specs = [pl.BlockSpec((tm, SEG), lambda j, i: (i, j)) for _ in out_dtypes]
    out_shape = [jax.ShapeDtypeStruct((m, nseg * SEG), dt) for dt in out_dtypes]
    if stack:
        out_specs.append(pl.BlockSpec((n_prev + 1, tm, SEG // HD, HD), lambda j, i: (0, i, 0, 0)))
        out_shape.append(jax.ShapeDtypeStruct((n_prev + 1, m, SEG // HD, HD), F32))
    kern = functools.partial(_proj_kernel, shift=shift, n_out=len(out_dtypes), n_prev=n_prev, stack=stack,
                             out_scale=out_scale)
    return pl.pallas_call(
        kern,
        grid=(nseg, m // tm),
        in_specs=in_specs,
        out_specs=out_specs,
        out_shape=out_shape,
        scratch_shapes=[pltpu.VMEM((k, SEG), BF16)],
        compiler_params=_params(("parallel", "arbitrary")),
        name="proj",
    )(*args)


def _gates_kernel(x_ref, wf_ref, wa_ref, wb_ref, bias_ref, alog_ref, logf_ref, col_ref, row_ref, qa_ref, ka_ref,
                  carry_ref, *, sb, chunk, nh):
    t = pl.program_id(1)
    tb = x_ref.shape[0]

    @pl.when(t == 0)
    def _():
        carry_ref[...] = jnp.zeros_like(carry_ref)

    lane = lax.broadcasted_iota(jnp.int32, (1, LANES), 1)
    wt = jnp.concatenate([wf_ref[...], wa_ref[...], wb_ref[...],
                          jnp.zeros((LANES - 3 * nh, wf_ref.shape[1]), F32)], axis=0).astype(BF16)
    raw = _dot_nt(x_ref[...], wt) + bias_ref[...]
    sp_neg = jnp.maximum(-raw, 0.0) + jnp.log1p(jnp.exp(-jnp.abs(raw)))
    sp_pos = jnp.maximum(raw, 0.0) + jnp.log1p(jnp.exp(-jnp.abs(raw)))
    logf = -sp_neg
    g = -jnp.exp(alog_ref[...]) * sp_pos
    beta = 1.0 / (1.0 + jnp.exp(-raw))
    is_f = lane < nh
    is_g = (lane >= nh) & (lane < 2 * nh)
    is_b = (lane >= 2 * nh) & (lane < 3 * nh)
    logf_ref[...] = logf[:, :nh]

    r = lax.broadcasted_iota(jnp.int32, (sb, sb), 0)
    c = lax.broadcasted_iota(jnp.int32, (sb, sb), 1)
    tri_full = (r >= c).astype(BF16)
    tri_chunk = ((r >= c) & ((r // chunk) == (c // chunk))).astype(BF16)
    er = lax.broadcasted_iota(jnp.int32, (32, LANES), 0)
    ec = lax.broadcasted_iota(jnp.int32, (32, LANES), 1)
    sel = (er == ec).astype(BF16)
    wide = nh * HD
    gr = lax.broadcasted_iota(jnp.int32, (LANES, wide), 0)
    gc = lax.broadcasted_iota(jnp.int32, (LANES, wide), 1)
    gh, gj = gr % nh, gr // nh
    spread_q = ((gc == gh * HD + 3 + gj) & (gj < 3)).astype(BF16)
    spread_k = ((gc == gh * HD + gj) & (gj < 3)).astype(BF16)
    lane_in_head = lax.broadcasted_iota(jnp.int32, (1, wide), 1) % HD
    ones_q = (lane_in_head < 3).astype(F32)
    ones_k = ((lane_in_head >= 3) & (lane_in_head < 6)).astype(F32)

    carry = carry_ref[...]
    for s in range(tb // sb):
        rows = slice(s * sb, (s + 1) * sb)
        lf = jnp.where(is_f, logf[rows], 0.0)
        gg = jnp.where(is_g, g[rows], 0.0)
        cs = jnp.zeros((sb, LANES), F32)
        for part in _split3(lf):
            cs = cs + _dot(tri_full, part)
        for part in _split3(gg):
            cs = cs + _dot(tri_chunk, part)
        cs = cs + carry
        carry = jnp.where(is_f, cs[sb - 1:sb, :], 0.0)
        colv = jnp.where(is_b, beta[rows], cs)
        col_ref[rows, :] = colv
        rv = jnp.zeros((32, sb), F32)
        terms = jnp.zeros((sb, LANES), F32)
        for j, part in enumerate(_split3(colv)):
            rv = rv + _dot_nt(sel, part)
            pf = jnp.where(is_f, part.astype(F32), 0.0)
            terms = terms + (pf if j == 0 else pltpu.roll(pf, j * nh, 1))
        tb16 = terms.astype(BF16)
        row_ref[:, rows] = rv
        qa_ref[rows, :] = (ones_q + _dot(tb16, spread_q)).astype(BF16)
        ka_ref[rows, :] = (ones_k - _dot(tb16, spread_k)).astype(BF16)
    carry_ref[...] = carry


def gates(xb, wt, layer, bias_row, alog_row, bsz, t_len, chunk, nh):
    m, d = xb.shape
    tb = _pick(t_len, (512, 256, 128, 64, 32, 16))
    sb = min(tb, 128)
    nt = t_len // tb
    width = nh * HD
    assert nh == 8 and 3 * nh <= LANES
    f_blk = 3 * width // nh
    a_blk = (6 * width + nh) // nh
    kern = functools.partial(_gates_kernel, sb=sb, chunk=chunk, nh=nh)
    return pl.pallas_call(
        kern,
        grid=(bsz, nt),
        in_specs=[pl.BlockSpec((tb, d), lambda b, t: (b * nt + t, 0)),
                  pl.BlockSpec((None, nh, d), lambda b, t: (layer, f_blk, 0)),
                  pl.BlockSpec((None, nh, d), lambda b, t: (layer, a_blk, 0)),
                  pl.BlockSpec((None, nh, d), lambda b, t: (layer, a_blk + 1, 0)),
                  pl.BlockSpec((1, LANES), lambda b, t: (0, 0)),
                  pl.BlockSpec((1, LANES), lambda b, t: (0, 0))],
        out_specs=[pl.BlockSpec((None, tb, nh), lambda b, t: (b, t, 0)),
                   pl.BlockSpec((None, tb, LANES), lambda b, t: (b, t, 0)),
                   pl.BlockSpec((None, 32, tb), lambda b, t: (b, 0, t)),
                   pl.BlockSpec((tb, width), lambda b, t: (b * nt + t, 0)),
                   pl.BlockSpec((tb, width), lambda b, t: (b * nt + t, 0))],
        out_shape=[jax.ShapeDtypeStruct((bsz, t_len, nh), F32),
                   jax.ShapeDtypeStruct((bsz, t_len, LANES), F32),
                   jax.ShapeDtypeStruct((bsz, 32, t_len), F32),
                   jax.ShapeDtypeStruct((m, width), BF16),
                   jax.ShapeDtypeStruct((m, width), BF16)],
        scratch_shapes=[pltpu.VMEM((1, LANES), F32)],
        compiler_params=_params(("parallel", "arbitrary")),
        name="gates",
    )(xb, wt, wt, wt, bias_row, alog_row)


def _fox_prompt_kernel(q_ref, qa_ref, k_ref, ka_ref, v_ref, nw_ref, o_ref, q2_ref, m_ref, acc_ref, *, nh, tq):
    qi = pl.program_id(1)
    nw = nw_ref[...]
    r = lax.broadcasted_iota(jnp.int32, (tq, LANES), 0)
    c = lax.broadcasted_iota(jnp.int32, (tq, LANES), 1)
    heads = [slice(h * HD, (h + 1) * HD) for h in range(nh)]
    for h, cols in enumerate(heads):
        q2_ref[h] = jnp.concatenate([q_ref[:, cols], qa_ref[:, cols]], axis=-1)

    def scores(start, width, group, masked):
        out = []
        for h in group:
            cols = heads[h]
            k2 = jnp.concatenate([k_ref[pl.ds(start, width), cols], ka_ref[pl.ds(start, width), cols]], axis=-1)
            s = _dot_nt(q2_ref[h], k2)
            pieces = [s[:, k * LANES:(k + 1) * LANES] for k in range(width // LANES)]
            if masked:
                pieces = [jnp.where(r >= c + k * LANES, sk, -jnp.inf) for k, sk in enumerate(pieces)]
            out.append(pieces)
        return out

    def row_max(pieces):
        mx = pieces[0]
        for sk in pieces[1:]:
            mx = jnp.maximum(mx, sk)
        return jnp.broadcast_to(jnp.max(mx, axis=-1, keepdims=True), (tq, LANES))

    def pv(p_pieces, start, width, h):
        p = jnp.concatenate(p_pieces, axis=-1).astype(BF16)
        v_aug = jnp.concatenate([v_ref[pl.ds(start, width), heads[h]], jnp.ones((width, LANES), BF16)], axis=-1)
        return _dot(p, v_aug)

    groups = [list(range(g, min(g + HEAD_GROUP, nh))) for g in range(0, nh, HEAD_GROUP)]

    def block(start, width, first):
        pending = scores(start, width, groups[0], first)
        for gi, group in enumerate(groups):
            ss = pending
            if gi + 1 < len(groups):
                pending = scores(start, width, groups[gi + 1], first)
            ps, alphas = [], []
            for h, pieces in zip(group, ss):
                mx = row_max(pieces)
                if first:
                    m_new = mx
                else:
                    m = m_ref[h]
                    m_new = jnp.maximum(m, mx)
                    alphas.append(jnp.exp(m - m_new))
                m_ref[h] = m_new
                ps.append([jnp.exp(sk - m_new) for sk in pieces])
            for i, (h, pp) in enumerate(zip(group, ps)):
                if first:
                    acc_ref[h] = pv(pp, start, width, h)
                else:
                    a = alphas[i]
                    acc_ref[h] = jnp.concatenate([a, a], axis=-1) * acc_ref[h] + pv(pp, start, width, h)

    block(pl.multiple_of(qi * tq, tq), tq, True)

    @pl.when(qi % 2 == 1)
    def _():
        block(pl.multiple_of((qi - 1) * tq, tq), tq, False)

    def step(j, carry):
        block(pl.multiple_of(j * (2 * tq), 2 * tq), 2 * tq, False)
        return carry

    lax.fori_loop(0, qi // 2, step, 0)
    for h in range(nh):
        acc = acc_ref[h]
        o = acc[:, :HD] / acc[:, HD:]
        o = o * lax.rsqrt(jnp.mean(o * o, axis=-1, keepdims=True) + RMS_EPS) * nw
        o_ref[:, heads[h]] = o.astype(o_ref.dtype)


def fox_prompt(qb, qa, kb, ka, vb, norm_w, bsz, t_len, nh):
    tq = _pick(t_len, (512, 256, 128))
    assert tq % LANES == 0 and t_len % tq == 0
    nq = t_len // tq
    w = nh * HD
    kern = functools.partial(_fox_prompt_kernel, nh=nh, tq=tq)
    qspec = pl.BlockSpec((tq, w), lambda b, i: (b * nq + i, 0))
    kspec = pl.BlockSpec((t_len, w), lambda b, i: (b, 0))
    return pl.pallas_call(
        kern,
        grid=(bsz, nq),
        in_specs=[qspec, qspec, kspec, kspec, kspec, pl.BlockSpec((1, HD), lambda b, i: (0, 0))],
        out_specs=pl.BlockSpec((tq, w), lambda b, i: (b * nq + i, 0)),
        out_shape=jax.ShapeDtypeStruct((bsz * t_len, w), BF16),
        scratch_shapes=[pltpu.VMEM((nh, tq, 2 * HD), BF16), pltpu.VMEM((nh, tq, LANES), F32),
                        pltpu.VMEM((nh, tq, 2 * HD), F32)],
        compiler_params=_params(("parallel", "parallel")),
        name="fox_prompt",
    )(qb, qa, kb, ka, vb, norm_w)


def _cache_gate_kernel(lf_ref, o_ref):
    past = lf_ref.shape[1]
    r = lax.broadcasted_iota(jnp.int32, (past, past), 0)
    c = lax.broadcasted_iota(jnp.int32, (past, past), 1)
    later = (r > c).astype(BF16)
    acc = jnp.zeros(lf_ref.shape, F32)
    for part in _split3(lf_ref[...]):
        acc = acc + _dot(part, later)
    o_ref[...] = acc


def cache_gate_suffix(lf_t):
    bsz, nh, past = lf_t.shape
    rows = bsz * nh
    return pl.pallas_call(
        _cache_gate_kernel,
        grid=(1,),
        in_specs=[pl.BlockSpec((rows, past), lambda i: (0, 0))],
        out_specs=pl.BlockSpec((rows, past), lambda i: (0, 0)),
        out_shape=jax.ShapeDtypeStruct((rows, past), F32),
        compiler_params=_params(("arbitrary",)),
        name="cache_gate_suffix",
    )(lf_t.reshape(rows, past)).reshape(bsz, nh, past)


def _fox_sample_kernel(q_ref, kn_ref, vn_ref, kc_ref, vc_ref, sfx_ref, col_ref, row_ref, nw_ref, o_ref, *, nh):
    t_len = q_ref.shape[0]
    past = kc_ref.shape[0]
    nw = nw_ref[...]
    heads = [slice(h * HD, (h + 1) * HD) for h in range(nh)]
    q_all = jnp.concatenate([q_ref[:, cols] for cols in heads], axis=0)
    cn_all = jnp.concatenate([col_ref[:, h:h + 1] for h in range(nh)], axis=0)
    k_flat = kc_ref[...].reshape(past * nh, HD).astype(BF16)
    v_flat = vc_ref[...].reshape(past * nh, HD).astype(BF16)
    rr = lax.broadcasted_iota(jnp.int32, (nh * t_len, past * nh), 0)
    cc = lax.broadcasted_iota(jnp.int32, (nh * t_len, past * nh), 1)
    own_head = (cc % nh) == (rr // t_len)
    s_c = jnp.where(own_head, _dot_nt(q_all, k_flat) + (cn_all + sfx_ref[...]), -jnp.inf)
    tr = lax.broadcasted_iota(jnp.int32, (t_len, t_len), 0)
    tc = lax.broadcasted_iota(jnp.int32, (t_len, t_len), 1)
    causal = tr >= tc
    s_n = []
    for h, cols in enumerate(heads):
        sn = _dot_nt(q_ref[:, cols], kn_ref[:, cols]) + (col_ref[:, h:h + 1] - row_ref[h:h + 1, :])
        s_n.append(jnp.where(causal, sn, -jnp.inf))
    m_n = jnp.concatenate([jnp.max(sn, axis=-1, keepdims=True) for sn in s_n], axis=0)
    m = jnp.maximum(jnp.max(s_c, axis=-1, keepdims=True), m_n)
    p_c = jnp.exp(s_c - m)
    l_c = jnp.sum(p_c, axis=-1, keepdims=True)
    o_c = _dot(p_c.astype(BF16), v_flat)
    for h, cols in enumerate(heads):
        rows = slice(h * t_len, (h + 1) * t_len)
        p_n = jnp.exp(s_n[h] - m[rows])
        l = l_c[rows] + jnp.sum(p_n, axis=-1, keepdims=True)
        o = (o_c[rows] + _dot(p_n.astype(BF16), vn_ref[:, cols])) / l
        o = o * lax.rsqrt(jnp.mean(o * o, axis=-1, keepdims=True) + RMS_EPS) * nw
        o_ref[:, cols] = o.astype(o_ref.dtype)


def fox_sample(qb, knb, vnb, k_cache, v_cache, layer, suffix_flat, col, row, norm_w, bsz, t_len, nh):
    w = nh * HD
    past = k_cache.shape[2]
    kern = functools.partial(_fox_sample_kernel, nh=nh)
    return pl.pallas_call(
        kern,
        grid=(bsz,),
        in_specs=[pl.BlockSpec((t_len, w), lambda b: (b, 0)),
                  pl.BlockSpec((t_len, w), lambda b: (b, 0)),
                  pl.BlockSpec((t_len, w), lambda b: (b, 0)),
                  pl.BlockSpec((None, None, past, nh, HD), lambda b: (layer, b, 0, 0, 0)),
                  pl.BlockSpec((None, None, past, nh, HD), lambda b: (layer, b, 0, 0, 0)),
                  pl.BlockSpec((None, 1, past * nh), lambda b: (b, 0, 0)),
                  pl.BlockSpec((None, t_len, LANES), lambda b: (b, 0, 0)),
                  pl.BlockSpec((None, 32, t_len), lambda b: (b, 0, 0)),
                  pl.BlockSpec((1, HD), lambda b: (0, 0))],
        out_specs=pl.BlockSpec((t_len, w), lambda b: (b, 0)),
        out_shape=jax.ShapeDtypeStruct((bsz * t_len, w), BF16),
        compiler_params=_params(("parallel",)),
        name="fox_sample",
    )(qb, knb, vnb, k_cache, v_cache, suffix_flat, col, row, norm_w)


def _gdn_prep_kernel(raw_ref, prev_ref, cbuf_ref, cw_ref, col_ref, row_ref,
                     u_ref, w_ref, qg_ref, kd_ref, qk_ref, ext_ref, *, nh, lc):
    t = pl.program_id(1)
    tb = raw_ref.shape[0]
    width = nh * HD
    pad = 8
    @pl.when(t == 0)
    def _():
        ext_ref[0:pad, :] = cbuf_ref[...]

    @pl.when(t > 0)
    def _():
        ext_ref[0:pad, :] = prev_ref[...]

    ext_ref[pad:pad + tb, :] = raw_ref[...]

    r = lax.broadcasted_iota(jnp.int32, (lc, lc), 0)
    c = lax.broadcasted_iota(jnp.int32, (lc, lc), 1)
    eye = (r == c).astype(F32)
    tril_f = (r >= c).astype(F32)
    offdiag_f = 1.0 - eye
    n_sq = int(math.log2(lc)) - 1
    qscale = HD ** -0.5
    zpad = jnp.zeros((lc, HD - lc), BF16) if lc < HD else None

    def conv_act(seg, h):
        lo = seg * width + h * HD
        acc = None
        for i in reversed(range(CONV_W)):
            term = ext_ref[pl.ds(pad - (CONV_W - 1) + i, tb), lo:lo + HD] * cw_ref[i:i + 1, lo:lo + HD]
            acc = term if acc is None else acc + term
        return _silu(acc)

    def l2n(x):
        return x * lax.rsqrt(jnp.sum(x * x, axis=-1, keepdims=True) + L2_EPS)

    units = []
    for h in range(nh):
        cols = slice(h * HD, (h + 1) * HD)
        q_all = l2n(conv_act(0, h)) * qscale
        k_all = l2n(conv_act(1, h))
        v_all = conv_act(2, h)
        for ci in range(tb // lc):
            rows = slice(ci * lc, (ci + 1) * lc)
            q = q_all[rows]
            k = k_all[rows]
            g_col = col_ref[rows, nh + h:nh + h + 1]
            b_col = col_ref[rows, 2 * nh + h:2 * nh + h + 1]
            g_row = row_ref[nh + h:nh + h + 1, rows]
            g_last = g_row[:, lc - 1:lc]
            kb = k * b_col
            qg_ref[rows, cols] = (q * jnp.exp(g_col)).astype(BF16)
            kd_ref[rows, cols] = (k * jnp.exp(g_last - g_col)).astype(BF16)
            units.append(dict(
                rows=rows, cols=cols,
                kbq=jnp.concatenate([kb, q], axis=0).astype(BF16),
                kbf=k.astype(BF16),
                decay=jnp.exp(jnp.minimum(g_col - g_row, 0.0)) * tril_f,
                rhs=jnp.concatenate([v_all[rows] * b_col, kb * jnp.exp(g_col)], axis=-1).astype(BF16)))

    for un in units:
        akq = _dot_nt(un["kbq"], un["kbf"])
        qkb = (akq[lc:] * un["decay"]).astype(BF16)
        if zpad is not None:
            qkb = jnp.concatenate([qkb, zpad], axis=-1)
        qk_ref[un["rows"], un["cols"]] = qkb
        un["p"] = -(akq[:lc] * (un["decay"] * offdiag_f))
        un["t"] = eye + un["p"]
    for un in units:
        pb = un["p"].astype(BF16)
        un["p"] = _dot(pb, pb)
    for _ in range(n_sq - 1):
        for un in units:
            st = _dot(jnp.concatenate([un["p"], un["t"]], axis=0).astype(BF16), un["p"].astype(BF16))
            un["p"] = st[:lc]
            un["t"] = un["t"] + st[lc:]
    for un in units:
        un["t"] = un["t"] + _dot(un["t"].astype(BF16), un["p"].astype(BF16))
    for un in units:
        uw = _dot(un["t"].astype(BF16), un["rhs"])
        u_ref[un["rows"], un["cols"]] = uw[:, :HD]
        w_ref[un["rows"], un["cols"]] = uw[:, HD:].astype(BF16)


def gdn_prep(raw, conv_state8, conv_w8, col, row, bsz, t_len, nh, lc):
    m = raw.shape[0]
    width = nh * HD
    c3 = 3 * width
    tb = _pick(t_len, (128, 64, 32, 16))
    nt = t_len // tb
    kern = functools.partial(_gdn_prep_kernel, nh=nh, lc=lc)
    blk = lambda b, t: (b * nt + t, 0)
    return pl.pallas_call(
        kern,
        grid=(bsz, nt),
        in_specs=[pl.BlockSpec((tb, c3), blk),
                  pl.BlockSpec((8, c3), lambda b, t: (jnp.maximum((b * nt + t) * (tb // 8) - 1, 0), 0)),
                  pl.BlockSpec((None, 8, c3), lambda b, t: (b, 0, 0)),
                  pl.BlockSpec((8, c3), lambda b, t: (0, 0)),
                  pl.BlockSpec((None, tb, LANES), lambda b, t: (b, t, 0)),
                  pl.BlockSpec((None, 32, tb), lambda b, t: (b, 0, t))],
        out_specs=[pl.BlockSpec((tb, width), blk) for _ in range(5)],
        out_shape=[jax.ShapeDtypeStruct((m, width), F32)] +
                  [jax.ShapeDtypeStruct((m, width), BF16) for _ in range(4)],
        scratch_shapes=[pltpu.VMEM((tb + 8, c3), F32)],
        compiler_params=_params(("parallel", "parallel")),
        name="gdn_prep",
    )(raw, raw, conv_state8, conv_w8, col, row)


def _gdn_scan_kernel(u_ref, w_ref, qg_ref, kd_ref, qk_ref, z_ref, row_ref, nw_ref, s0_ref,
                     o_ref, s_ref, *, nh, lc):
    t = pl.program_id(1)
    nb, tb = u_ref.shape[0], u_ref.shape[1]
    nw = nw_ref[...]

    @pl.when(t == 0)
    def _():
        s_ref[...] = s0_ref[...]

    units = [(p, h, slice(h * HD, (h + 1) * HD)) for p in range(nb) for h in range(nh)]
    states = [s_ref[p, h] for p, h, _ in units]
    for ci in range(tb // lc):
        rows = slice(ci * lc, (ci + 1) * lc)
        wqs = [_dot(jnp.concatenate([w_ref[p, rows, cols], qg_ref[p, rows, cols]], axis=0), states[n].astype(BF16))
               for n, (p, h, cols) in enumerate(units)]
        vbs = [(u_ref[p, rows, cols] - wqs[n][:lc]).astype(BF16) for n, (p, h, cols) in enumerate(units)]
        o2s = [_dot(qk_ref[p, rows, h * HD:h * HD + lc], vbs[n]) for n, (p, h, cols) in enumerate(units)]
        sds = [_dot_tn(kd_ref[p, rows, cols], vbs[n]) for n, (p, h, cols) in enumerate(units)]
        for n, (p, h, cols) in enumerate(units):
            g_last = row_ref[p, nh + h:nh + h + 1, ci * lc + lc - 1:ci * lc + lc]
            states[n] = states[n] * jnp.exp(g_last) + sds[n]
            o = wqs[n][lc:] + o2s[n]
            o = o * lax.rsqrt(jnp.mean(o * o, axis=-1, keepdims=True) + RMS_EPS) * nw * _silu(z_ref[p, rows, cols])
            o_ref[p, rows, cols] = o.astype(o_ref.dtype)
    for n, (p, h, _) in enumerate(units):
        s_ref[p, h] = states[n]


def gdn_scan(u, w, qg, kd, qk, z, row, norm_w, s0, bsz, t_len, nh, lc):
    m, width = u.shape
    tb = _pick(t_len, (256, 128, 64, 32, 16))
    nt = t_len // tb
    nb = 4 if bsz % 4 == 0 else (2 if bsz % 2 == 0 else 1)
    kern = functools.partial(_gdn_scan_kernel, nh=nh, lc=lc)
    rows3 = lambda a: a.reshape(bsz, t_len, width)
    blk = pl.BlockSpec((nb, tb, width), lambda b, t: (b, t, 0))
    mix, s_fin = pl.pallas_call(
        kern,
        grid=(bsz // nb, nt),
        in_specs=[blk for _ in range(6)] +
                 [pl.BlockSpec((nb, 32, tb), lambda b, t: (b, 0, t)),
                  pl.BlockSpec((1, HD), lambda b, t: (0, 0)),
                  pl.BlockSpec((nb, nh, HD, HD), lambda b, t: (b, 0, 0, 0))],
        out_specs=[blk, pl.BlockSpec((nb, nh, HD, HD), lambda b, t: (b, 0, 0, 0))],
        out_shape=[jax.ShapeDtypeStruct((bsz, t_len, width), BF16),
                   jax.ShapeDtypeStruct((bsz, nh, HD, HD), F32)],
        compiler_params=_params(("parallel", "arbitrary")),
        name="gdn_scan",
    )(rows3(u), rows3(w), rows3(qg), rows3(kd), rows3(qk), rows3(z), row, norm_w, s0)
    return mix.reshape(m, width), s_fin


def _out_ln_kernel(a_ref, b_ref, w_ref, res_ref, g_ref, beta_ref, o_ref, ob_ref, *, alpha):
    ka = a_ref.shape[1]
    y = _dot(a_ref[...], w_ref[0:ka, :]) + _dot(b_ref[...], w_ref[ka:, :])
    y = _ln_rows(alpha * res_ref[...] + y, g_ref[...], beta_ref[...])
    o_ref[...] = y
    ob_ref[...] = y.astype(BF16)


def out_proj_ln(mix_a, mix_b, w, res, g, beta, alpha):
    m, ka = mix_a.shape
    kb = mix_b.shape[1]
    d = w.shape[1]
    tm = _pick(m, (512, 256, 128, 64, 32, 16, 8))
    kern = functools.partial(_out_ln_kernel, alpha=alpha)
    row = lambda i: (i, 0)
    fixed = lambda i: (0, 0)
    return pl.pallas_call(
        kern,
        grid=(m // tm,),
        in_specs=[pl.BlockSpec((tm, ka), row), pl.BlockSpec((tm, kb), row),
                  pl.BlockSpec((ka + kb, d), fixed, pipeline_mode=pl.Buffered(1)),
                  pl.BlockSpec((tm, d), row),
                  pl.BlockSpec((1, d), fixed), pl.BlockSpec((1, d), fixed)],
        out_specs=[pl.BlockSpec((tm, d), row), pl.BlockSpec((tm, d), row)],
        out_shape=[jax.ShapeDtypeStruct((m, d), F32), jax.ShapeDtypeStruct((m, d), BF16)],
        compiler_params=_params(("parallel",)),
        name="out_proj_ln",
    )(mix_a, mix_b, w, res, g.reshape(1, d), beta.reshape(1, d))


def _ffn_up_kernel(*refs, cast_down):
    if cast_down:
        x_ref, wg_ref, wu_ref, wd_ref, h_ref, wdb_ref, wgb_ref, wub_ref = refs
    else:
        x_ref, wg_ref, wu_ref, h_ref, wgb_ref, wub_ref = refs

    @pl.when(pl.program_id(1) == 0)
    def _():
        for r0 in range(0, wg_ref.shape[0], CAST_ROWS):
            rows = slice(r0, r0 + CAST_ROWS)
            wgb_ref[rows, :] = wg_ref[rows, :].astype(BF16)
            wub_ref[rows, :] = wu_ref[rows, :].astype(BF16)
        if cast_down:
            wdb_ref[...] = wd_ref[...].astype(BF16)

    x = x_ref[...]
    gate = _dot(x, wgb_ref[...])
    up = _dot(x, wub_ref[...])
    h_ref[...] = (_silu(gate) * up).astype(h_ref.dtype)


def ffn_up(xb, wg, wu, layer, wd=None):
    m, d = xb.shape
    f = wg.shape[2]
    assert d % CAST_ROWS == 0
    tm = _pick(m, (1024, 512, 256, 128, 64, 32, 16, 8))
    tf = _pick(f, (512, 256, 128))
    wspec = pl.BlockSpec((None, d, tf), lambda j, i: (layer, 0, j))
    in_specs = [pl.BlockSpec((tm, d), lambda j, i: (i, 0)), wspec, wspec]
    out_specs = [pl.BlockSpec((tm, tf), lambda j, i: (i, j))]
    out_shape = [jax.ShapeDtypeStruct((m, f), BF16)]
    args = [xb, wg, wu]
    if wd is not None:
        dout = wd.shape[2]
        in_specs.append(pl.BlockSpec((None, tf, dout), lambda j, i: (layer, j, 0)))
        out_specs.append(pl.BlockSpec((tf, dout), lambda j, i: (j, 0)))
        out_shape.append(jax.ShapeDtypeStruct((f, dout), BF16))
        args.append(wd)
    return pl.pallas_call(
        functools.partial(_ffn_up_kernel, cast_down=wd is not None),
        grid=(f // tf, m // tm),
        in_specs=in_specs,
        out_specs=out_specs,
        out_shape=out_shape,
        scratch_shapes=[pltpu.VMEM((d, tf), BF16), pltpu.VMEM((d, tf), BF16)],
        compiler_params=_params(("parallel", "arbitrary")),
        name="ffn_up",
    )(*args)


def _ffn_down_kernel(h_ref, w_ref, res_ref, g_ref, beta_ref, o_ref, ob_ref, *, alpha):
    y = _dot(h_ref[...], w_ref[...])
    y = _ln_rows(alpha * res_ref[...] + y, g_ref[...], beta_ref[...])
    o_ref[...] = y
    ob_ref[...] = y.astype(BF16)


def ffn_down_ln(h, w, res, g, beta, alpha):
    m, f = h.shape
    d = w.shape[1]
    tm = _pick(m, (256, 128, 64, 32, 16, 8))
    kern = functools.partial(_ffn_down_kernel, alpha=alpha)
    row = lambda i: (i, 0)
    fixed = lambda i: (0, 0)
    return pl.pallas_call(
        kern,
        grid=(m // tm,),
        in_specs=[pl.BlockSpec((tm, f), row),
                  pl.BlockSpec((f, d), fixed, pipeline_mode=pl.Buffered(1)),
                  pl.BlockSpec((tm, d), row),
                  pl.BlockSpec((1, d), fixed), pl.BlockSpec((1, d), fixed)],
        out_specs=[pl.BlockSpec((tm, d), row), pl.BlockSpec((tm, d), row)],
        out_shape=[jax.ShapeDtypeStruct((m, d), F32), jax.ShapeDtypeStruct((m, d), BF16)],
        compiler_params=_params(("parallel",)),
        name="ffn_down_ln",
    )(h, w, res, g.reshape(1, d), beta.reshape(1, d))


def _layer(x, xb, bsz, t_len, layer, wts, lw, fox_cache, conv_state, s0, kv_prev, alpha):
    nh = lw["nh"]
    width = nh * HD
    lc = min(t_len, CHUNK)
    w_in = wts["w_in"]
    k_prev, v_prev = kv_prev if kv_prev is not None else (None, None)
    qb, = proj_segments(xb, w_in, layer, 0, 1, 0, (BF16,), out_scale=HD ** -0.5)
    kb, k_stack = proj_segments(xb, w_in, layer, 1, 1, 0, (BF16,), prev=k_prev, stack=True)
    vb, v_stack = proj_segments(xb, w_in, layer, 2, 1, 0, (BF16,), prev=v_prev, stack=True)
    raw, = proj_segments(xb, w_in, layer, 3, 3, nh, (F32,))
    z, = proj_segments(xb, w_in, layer, 6, 1, 3 * nh, (F32,))
    logf, col, row, qa, ka = gates(xb, w_in, layer, lw["gate_bias"], lw["alog_row"], bsz, t_len, lc, nh)

    if fox_cache is None:
        mix_f = fox_prompt(qb, qa, kb, ka, vb, lw["fox_norm_w"], bsz, t_len, nh)
    else:
        k_cache, v_cache, lf_cache = fox_cache
        sfx = cache_gate_suffix(jnp.transpose(lf_cache, (0, 2, 1)))
        sfx_flat = jnp.transpose(sfx, (0, 2, 1)).reshape(bsz, 1, -1)
        mix_f = fox_sample(qb, kb, vb, k_cache, v_cache, layer, sfx_flat, col, row,
                           lw["fox_norm_w"], bsz, t_len, nh)

    conv_state8 = jnp.pad(conv_state, ((0, 0), (8 - (CONV_W - 1), 0), (0, 0)))
    u, w, qg, kd, qk = gdn_prep(raw, conv_state8, lw["conv_w8"], col, row, bsz, t_len, nh, lc)
    mix_g, s_fin = gdn_scan(u, w, qg, kd, qk, z, row, lw["gdn_norm_w"], s0, bsz, t_len, nh, lc)

    x1, x1b = out_proj_ln(mix_f, mix_g, lw["w_out"], x, lw["ln_mix_g"], lw["ln_mix_b"], alpha)
    if "w_down" in lw:
        hmid, = ffn_up(x1b, wts["w_gate"], wts["w_up"], layer)
    else:
        hmid, lw["w_down"] = ffn_up(x1b, wts["w_gate"], wts["w_up"], layer, wd=wts["w_down"])
    x2, x2b = ffn_down_ln(hmid, lw["w_down"], x1, lw["ln_ffn_g"], lw["ln_ffn_b"], alpha)

    new_buf = raw.reshape(bsz, t_len, -1)[:, t_len - (CONV_W - 1):, :]
    return x2, x2b, (k_stack, v_stack), (logf, s_fin, new_buf)


def kernel(x_prompt, x_sample, cache_fox_k, cache_fox_v, cache_fox_logf, state_gdn, state_gdn_conv, ln_in_g, ln_in_b, w_in, fox_f_bias, fox_norm_w, gdn_conv_w, gdn_a_log, gdn_dt_bias, gdn_norm_w, w_out, ln_mix_g, ln_mix_b, ffn_w_gate, ffn_w_up, ffn_w_down, ln_ffn_g, ln_ffn_b):
    depth = w_in.shape[0]
    bp, tp, d = x_prompt.shape
    bs, ts, _ = x_sample.shape
    nh = fox_f_bias.shape[1]
    assert fox_norm_w.shape[1] == HD and gdn_norm_w.shape[1] == HD and gdn_a_log.shape[1] == nh
    width = nh * HD
    assert width == SEG and w_in.shape[2] == 7 * width + 3 * nh and gdn_conv_w.shape[2] == 3 * width
    alpha = (2 * depth) ** 0.25

    xp, xpb = layer_norm_rows(x_prompt.reshape(bp * tp, d), ln_in_g, ln_in_b)
    xs, xsb = layer_norm_rows(x_sample.reshape(bs * ts, d), ln_in_g, ln_in_b)

    zero_conv = jnp.zeros((bp, CONV_W - 1, 3 * width), F32)
    zero_state = jnp.zeros((bp, nh, HD, HD), F32)
    k_cache, v_cache = cache_fox_k, cache_fox_v
    wts = dict(w_in=jnp.swapaxes(w_in, 1, 2), w_gate=ffn_w_gate, w_up=ffn_w_up, w_down=ffn_w_down)
    kv_p = kv_s = None
    ps, ss = [], []
    for l in range(depth):
        gate_bias = jnp.concatenate([fox_f_bias[l], gdn_dt_bias[l], jnp.zeros((LANES - 2 * nh,), F32)]).reshape(1, LANES)
        alog_row = jnp.concatenate([jnp.zeros((nh,), F32), gdn_a_log[l], jnp.zeros((LANES - 2 * nh,), F32)]).reshape(1, LANES)
        lw = dict(
            nh=nh,
            gate_bias=gate_bias, alog_row=alog_row,
            fox_norm_w=fox_norm_w[l].reshape(1, HD),
            gdn_norm_w=gdn_norm_w[l].reshape(1, HD),
            conv_w8=jnp.pad(gdn_conv_w[l], ((0, 8 - CONV_W), (0, 0))),
            w_out=cast_bf16(w_out, l),
            ln_mix_g=ln_mix_g[l], ln_mix_b=ln_mix_b[l],
            ln_ffn_g=ln_ffn_g[l], ln_ffn_b=ln_ffn_b[l],
        )
        xp, xpb, kv_p, st_p = _layer(xp, xpb, bp, tp, l, wts, lw, None, zero_conv, zero_state, kv_p, alpha)
        xs, xsb, kv_s, st_s = _layer(xs, xsb, bs, ts, l, wts, lw, (k_cache, v_cache, cache_fox_logf[l]),
                                     state_gdn_conv[l], state_gdn[l], kv_s, alpha)
        ps.append(st_p)
        ss.append(st_s)

    def stacked(states, i):
        return jnp.stack([st[i] for st in states], axis=0)

    def heads(stack, bsz, t_len):
        return stack.reshape(depth, bsz, t_len, nh, HD)

    return (xp.reshape(bp, tp, d), xs.reshape(bs, ts, d),
            heads(kv_p[0], bp, tp), heads(kv_p[1], bp, tp), stacked(ps, 0), stacked(ps, 1), stacked(ps, 2),
            heads(kv_s[0], bs, ts), heads(kv_s[1], bs, ts), stacked(ss, 0), stacked(ss, 1), stacked(ss, 2))
```

```python
import functools
import math

import jax
import jax.numpy as jnp
from jax import lax
from jax.experimental import pallas as pl
from jax.experimental.pallas import tpu as pltpu

LN_EPS = 1e-5
RMS_EPS = 1e-6
L2_EPS = 1e-6
CHUNK = 64
CONV_W = 4
HD = 128
LANES = 128
VMEM_LIMIT = 56 * 1024 * 1024

F32 = jnp.float32
BF16 = jnp.bfloat16


def _pick(dim, cands):
    for c in cands:
        if c <= dim and dim % c == 0:
            return c
    return dim


def _params(sem):
    return pltpu.CompilerParams(dimension_semantics=sem, vmem_limit_bytes=VMEM_LIMIT)


def _split3(x):
    hi = x.astype(BF16)
    r1 = x - hi.astype(F32)
    mid = r1.astype(BF16)
    lo = (r1 - mid.astype(F32)).astype(BF16)
    return hi, mid, lo


def _dot(a, b):
    return jnp.dot(a, b, preferred_element_type=F32)


def _dot_nt(a, b):
    return lax.dot_general(a, b, (((1,), (1,)), ((), ())), preferred_element_type=F32)


def _dot_tn(a, b):
    return lax.dot_general(a, b, (((0,), (0,)), ((), ())), preferred_element_type=F32)


def _ln_rows(y, g, b):
    mu = jnp.mean(y, axis=-1, keepdims=True)
    d = y - mu
    var = jnp.mean(d * d, axis=-1, keepdims=True)
    return d * lax.rsqrt(var + LN_EPS) * g + b


def _silu(x):
    return x / (1.0 + jnp.exp(-x))


def _ln_kernel(x_ref, g_ref, b_ref, o_ref, ob_ref):
    y = _ln_rows(x_ref[...], g_ref[...], b_ref[...])
    o_ref[...] = y
    ob_ref[...] = y.astype(BF16)


def layer_norm_rows(x, g, b):
    m, d = x.shape
    tm = _pick(m, (512, 256, 128, 64, 32, 16, 8))
    return pl.pallas_call(
        _ln_kernel,
        grid=(m // tm,),
        in_specs=[pl.BlockSpec((tm, d), lambda i: (i, 0)),
                  pl.BlockSpec((1, d), lambda i: (0, 0)),
                  pl.BlockSpec((1, d), lambda i: (0, 0))],
        out_specs=[pl.BlockSpec((tm, d), lambda i: (i, 0)),
                   pl.BlockSpec((tm, d), lambda i: (i, 0))],
        out_shape=[jax.ShapeDtypeStruct((m, d), F32), jax.ShapeDtypeStruct((m, d), BF16)],
        compiler_params=_params(("parallel",)),
        name="ln_in",
    )(x, g.reshape(1, d), b.reshape(1, d))


def _cast_kernel(w_ref, o_ref):
    o_ref[...] = w_ref[...].astype(o_ref.dtype)


def cast_bf16(w, layer):
    _, k, n = w.shape
    tk = _pick(k, (512, 256, 128, 64, 32, 16))
    return pl.pallas_call(
        _cast_kernel,
        grid=(k // tk,),
        in_specs=[pl.BlockSpec((None, tk, n), lambda i: (layer, i, 0))],
        out_specs=pl.BlockSpec((tk, n), lambda i: (i, 0)),
        out_shape=jax.ShapeDtypeStruct((k, n), BF16),
        compiler_params=_params(("parallel",)),
        name="cast_bf16",
    )(w)


HEAD_GROUP = 1
SEG = 1024
CAST_ROWS = 256


def _proj_kernel(*refs, shift, n_out, n_prev, stack, out_scale):
    it = iter(refs)
    x_ref = next(it)
    wm_ref = next(it)
    wn_ref = next(it) if shift else None
    prev_ref = next(it) if n_prev else None
    o_refs = [next(it) for _ in range(n_out)]
    stack_ref = next(it) if stack else None
    wb_ref = next(it)

    @pl.when(pl.program_id(1) == 0)
    def _():
        for c0 in range(0, wm_ref.shape[1], CAST_ROWS):
            cols = slice(c0, c0 + CAST_ROWS)
            wm = wm_ref[:, cols]
            if shift:
                wm = jnp.concatenate([wm[shift:, :], wn_ref[0:shift, cols]], axis=0)
            wb_ref[cols, :] = wm.T.astype(BF16)

    acc = _dot(x_ref[...], wb_ref[...])
    for o in o_refs:
        o[...] = (acc if out_scale is None else acc * out_scale).astype(o.dtype)
    if stack:
        for l in range(n_prev):
            stack_ref[l] = prev_ref[l]
        stack_ref[n_prev] = acc.reshape(acc.shape[0], SEG // HD, HD)


def proj_segments(xb, wt, layer, seg0, nseg, shift, out_dtypes, prev=None, stack=False, out_scale=None):
    m, k = xb.shape
    assert k % CAST_ROWS == 0 and not (stack and nseg != 1) and shift % 8 == 0
    nb = 8 if shift <= 8 else 32
    assert shift <= nb and SEG % nb == 0
    n_prev = 0 if prev is None else prev.shape[0]
    tm = _pick(m, (1024, 512, 256, 128, 64, 32, 16, 8))
    single = pl.Buffered(1) if nseg == 1 else None
    in_specs = [pl.BlockSpec((tm, k), lambda j, i: (i, 0)),
                pl.BlockSpec((None, SEG, k), lambda j, i: (layer, seg0 + j, 0), pipeline_mode=single)]
    args = [xb, wt]
    if shift:
        in_specs.append(pl.BlockSpec((None, nb, k), lambda j, i: (layer, (seg0 + j + 1) * (SEG // nb), 0),
                                     pipeline_mode=single))
        args.append(wt)
    if n_prev:
        in_specs.append(pl.BlockSpec((n_prev, tm, SEG // HD, HD), lambda j, i: (0, i, 0, 0)))
        args.append(prev)
    out_specs = [pl.BlockSpec((tm, SEG), lambda j, i: (i, j)) for _ in out_dtypes]
    out_shape = [jax.ShapeDtypeStruct((m, nseg * SEG), dt) for dt in out_dtypes]
    if stack:
        out_specs.append(pl.BlockSpec((n_prev + 1, tm, SEG // HD, HD), lambda j, i: (0, i, 0, 0)))
        out_shape.append(jax.ShapeDtypeStruct((n_prev + 1, m, SEG // HD, HD), F32))
    kern = functools.partial(_proj_kernel, shift=shift, n_out=len(out_dtypes), n_prev=n_prev, stack=stack,
                             out_scale=out_scale)
    return pl.pallas_call(
        kern,
        grid=(nseg, m // tm),
        in_specs=in_specs,
        out_specs=out_specs,
        out_shape=out_shape,
        scratch_shapes=[pltpu.VMEM((k, SEG), BF16)],
        compiler_params=_params(("parallel", "arbitrary")),
        name="proj",
    )(*args)


def _gates_kernel(x_ref, wf_ref, wa_ref, wb_ref, bias_ref, alog_ref, logf_ref, col_ref, row_ref, qa_ref, ka_ref,
                  carry_ref, *, sb, chunk, nh):
    t = pl.program_id(1)
    tb = x_ref.shape[0]

    @pl.when(t == 0)
    def _():
        carry_ref[...] = jnp.zeros_like(carry_ref)

    lane = lax.broadcasted_iota(jnp.int32, (1, LANES), 1)
    wt = jnp.concatenate([wf_ref[...], wa_ref[...], wb_ref[...],
                          jnp.zeros((LANES - 3 * nh, wf_ref.shape[1]), F32)], axis=0).astype(BF16)
    raw = _dot_nt(x_ref[...], wt) + bias_ref[...]
    sp_neg = jnp.maximum(-raw, 0.0) + jnp.log1p(jnp.exp(-jnp.abs(raw)))
    sp_pos = jnp.maximum(raw, 0.0) + jnp.log1p(jnp.exp(-jnp.abs(raw)))
    logf = -sp_neg
    g = -jnp.exp(alog_ref[...]) * sp_pos
    beta = 1.0 / (1.0 + jnp.exp(-raw))
    is_f = lane < nh
    is_g = (lane >= nh) & (lane < 2 * nh)
    is_b = (lane >= 2 * nh) & (lane < 3 * nh)
    logf_ref[...] = logf[:, :nh]

    r = lax.broadcasted_iota(jnp.int32, (sb, sb), 0)
    c = lax.broadcasted_iota(jnp.int32, (sb, sb), 1)
    tri_full = (r >= c).astype(BF16)
    tri_chunk = ((r >= c) & ((r // chunk) == (c // chunk))).astype(BF16)
    er = lax.broadcasted_iota(jnp.int32, (32, LANES), 0)
    ec = lax.broadcasted_iota(jnp.int32, (32, LANES), 1)
    sel = (er == ec).astype(BF16)
    wide = nh * HD
    gr = lax.broadcasted_iota(jnp.int32, (LANES, wide), 0)
    gc = lax.broadcasted_iota(jnp.int32, (LANES, wide), 1)
    gh, gj = gr % nh, gr // nh
    spread_q = ((gc == gh * HD + 3 + gj) & (gj < 3)).astype(BF16)
    spread_k = ((gc == gh * HD + gj) & (gj < 3)).astype(BF16)
    lane_in_head = lax.broadcasted_iota(jnp.int32, (1, wide), 1) % HD
    ones_q = (lane_in_head < 3).astype(F32)
    ones_k = ((lane_in_head >= 3) & (lane_in_head < 6)).astype(F32)

    nsb = tb // sb
    blocks = [slice(s * sb, (s + 1) * sb) for s in range(nsb)]
    local = []
    for rows in blocks:
        cs = jnp.zeros((sb, LANES), F32)
        for part in _split3(jnp.where(is_f, logf[rows], 0.0)):
            cs = cs + _dot(tri_full, part)
        for part in _split3(jnp.where(is_g, g[rows], 0.0)):
            cs = cs + _dot(tri_chunk, part)
        local.append(cs)
    carry = carry_ref[...]
    colvs = []
    for rows, cs in zip(blocks, local):
        cs = cs + carry
        carry = jnp.where(is_f, cs[sb - 1:sb, :], 0.0)
        colv = jnp.where(is_b, beta[rows], cs)
        col_ref[rows, :] = colv
        colvs.append(colv)
    for rows, colv in zip(blocks, colvs):
        rv = jnp.zeros((32, sb), F32)
        terms = jnp.zeros((sb, LANES), F32)
        for j, part in enumerate(_split3(colv)):
            rv = rv + _dot_nt(sel, part)
            pf = jnp.where(is_f, part.astype(F32), 0.0)
            terms = terms + (pf if j == 0 else pltpu.roll(pf, j * nh, 1))
        tb16 = terms.astype(BF16)
        row_ref[:, rows] = rv
        qa_ref[rows, :] = (ones_q + _dot(tb16, spread_q)).astype(BF16)
        ka_ref[rows, :] = (ones_k - _dot(tb16, spread_k)).astype(BF16)
    carry_ref[...] = carry


def gates(xb, wt, layer, bias_row, alog_row, bsz, t_len, chunk, nh):
    m, d = xb.shape
    tb = _pick(t_len, (512, 256, 128, 64, 32, 16))
    sb = min(tb, 128)
    nt = t_len // tb
    width = nh * HD
    assert nh == 8 and 3 * nh <= LANES
    f_blk = 3 * width // nh
    a_blk = (6 * width + nh) // nh
    kern = functools.partial(_gates_kernel, sb=sb, chunk=chunk, nh=nh)
    return pl.pallas_call(
        kern,
        grid=(bsz, nt),
        in_specs=[pl.BlockSpec((tb, d), lambda b, t: (b * nt + t, 0)),
                  pl.BlockSpec((None, nh, d), lambda b, t: (layer, f_blk, 0)),
                  pl.BlockSpec((None, nh, d), lambda b, t: (layer, a_blk, 0)),
                  pl.BlockSpec((None, nh, d), lambda b, t: (layer, a_blk + 1, 0)),
                  pl.BlockSpec((1, LANES), lambda b, t: (0, 0)),
                  pl.BlockSpec((1, LANES), lambda b, t: (0, 0))],
        out_specs=[pl.BlockSpec((None, tb, nh), lambda b, t: (b, t, 0)),
                   pl.BlockSpec((None, tb, LANES), lambda b, t: (b, t, 0)),
                   pl.BlockSpec((None, 32, tb), lambda b, t: (b, 0, t)),
                   pl.BlockSpec((tb, width), lambda b, t: (b * nt + t, 0)),
                   pl.BlockSpec((tb, width), lambda b, t: (b * nt + t, 0))],
        out_shape=[jax.ShapeDtypeStruct((bsz, t_len, nh), F32),
                   jax.ShapeDtypeStruct((bsz, t_len, LANES), F32),
                   jax.ShapeDtypeStruct((bsz, 32, t_len), F32),
                   jax.ShapeDtypeStruct((m, width), BF16),
                   jax.ShapeDtypeStruct((m, width), BF16)],
        scratch_shapes=[pltpu.VMEM((1, LANES), F32)],
        compiler_params=_params(("parallel", "arbitrary")),
        name="gates",
    )(xb, wt, wt, wt, bias_row, alog_row)


def _fox_prompt_kernel(q_ref, qa_ref, k_ref, ka_ref, v_ref, nw_ref, o_ref, q2_ref, m_ref, acc_ref, *, nh, tq):
    qi = pl.program_id(1)
    nw = nw_ref[...]
    r = lax.broadcasted_iota(jnp.int32, (tq, LANES), 0)
    c = lax.broadcasted_iota(jnp.int32, (tq, LANES), 1)
    heads = [slice(h * HD, (h + 1) * HD) for h in range(nh)]
    for h, cols in enumerate(heads):
        q2_ref[h] = jnp.concatenate([q_ref[:, cols], qa_ref[:, cols]], axis=-1)

    def scores(start, width, group, masked):
        out = []
        for h in group:
            cols = heads[h]
            k2 = jnp.concatenate([k_ref[pl.ds(start, width), cols], ka_ref[pl.ds(start, width), cols]], axis=-1)
            s = _dot_nt(q2_ref[h], k2)
            pieces = [s[:, k * LANES:(k + 1) * LANES] for k in range(width // LANES)]
            if masked:
                pieces = [jnp.where(r >= c + k * LANES, sk, -jnp.inf) for k, sk in enumerate(pieces)]
            out.append(pieces)
        return out

    def row_max(pieces):
        mx = pieces[0]
        for sk in pieces[1:]:
            mx = jnp.maximum(mx, sk)
        return jnp.broadcast_to(jnp.max(mx, axis=-1, keepdims=True), (tq, LANES))

    def pv(p_pieces, start, width, h):
        p = jnp.concatenate(p_pieces, axis=-1).astype(BF16)
        v_aug = jnp.concatenate([v_ref[pl.ds(start, width), heads[h]], jnp.ones((width, LANES), BF16)], axis=-1)
        return _dot(p, v_aug)

    groups = [list(range(g, min(g + HEAD_GROUP, nh))) for g in range(0, nh, HEAD_GROUP)]

    def block(start, width, first):
        pending = scores(start, width, groups[0], first)
        for gi, group in enumerate(groups):
            ss = pending
            if gi + 1 < len(groups):
                pending = scores(start, width, groups[gi + 1], first)
            ps, alphas = [], []
            for h, pieces in zip(group, ss):
                mx = row_max(pieces)
                if first:
                    m_new = mx
                else:
                    m = m_ref[h]
                    m_new = jnp.maximum(m, mx)
                    alphas.append(jnp.exp(m - m_new))
                m_ref[h] = m_new
                ps.append([jnp.exp(sk - m_new) for sk in pieces])
            for i, (h, pp) in enumerate(zip(group, ps)):
                if first:
                    acc_ref[h] = pv(pp, start, width, h)
                else:
                    a = alphas[i]
                    acc_ref[h] = jnp.concatenate([a, a], axis=-1) * acc_ref[h] + pv(pp, start, width, h)

    block(pl.multiple_of(qi * tq, tq), tq, True)

    @pl.when(qi % 2 == 1)
    def _():
        block(pl.multiple_of((qi - 1) * tq, tq), tq, False)

    def step(j, carry):
        block(pl.multiple_of(j * (2 * tq), 2 * tq), 2 * tq, False)
        return carry

    lax.fori_loop(0, qi // 2, step, 0)
    for h in range(nh):
        acc = acc_ref[h]
        o = acc[:, :HD] / acc[:, HD:]
        o = o * lax.rsqrt(jnp.mean(o * o, axis=-1, keepdims=True) + RMS_EPS) * nw
        o_ref[:, heads[h]] = o.astype(o_ref.dtype)


def fox_prompt(qb, qa, kb, ka, vb, norm_w, bsz, t_len, nh):
    tq = _pick(t_len, (512, 256, 128))
    assert tq % LANES == 0 and t_len % tq == 0
    nq = t_len // tq
    w = nh * HD
    kern = functools.partial(_fox_prompt_kernel, nh=nh, tq=tq)
    qspec = pl.BlockSpec((tq, w), lambda b, i: (b * nq + i, 0))
    kspec = pl.BlockSpec((t_len, w), lambda b, i: (b, 0))
    return pl.pallas_call(
        kern,
        grid=(bsz, nq),
        in_specs=[qspec, qspec, kspec, kspec, kspec, pl.BlockSpec((1, HD), lambda b, i: (0, 0))],
        out_specs=pl.BlockSpec((tq, w), lambda b, i: (b * nq + i, 0)),
        out_shape=jax.ShapeDtypeStruct((bsz * t_len, w), BF16),
        scratch_shapes=[pltpu.VMEM((nh, tq, 2 * HD), BF16), pltpu.VMEM((nh, tq, LANES), F32),
                        pltpu.VMEM((nh, tq, 2 * HD), F32)],
        compiler_params=_params(("parallel", "parallel")),
        name="fox_prompt",
    )(qb, qa, kb, ka, vb, norm_w)


def _cache_gate_kernel(lf_ref, o_ref):
    past = lf_ref.shape[1]
    r = lax.broadcasted_iota(jnp.int32, (past, past), 0)
    c = lax.broadcasted_iota(jnp.int32, (past, past), 1)
    later = (r > c).astype(BF16)
    acc = jnp.zeros(lf_ref.shape, F32)
    for part in _split3(lf_ref[...]):
        acc = acc + _dot(part, later)
    o_ref[...] = acc


def cache_gate_suffix(lf_t):
    bsz, nh, past = lf_t.shape
    rows = bsz * nh
    return pl.pallas_call(
        _cache_gate_kernel,
        grid=(1,),
        in_specs=[pl.BlockSpec((rows, past), lambda i: (0, 0))],
        out_specs=pl.BlockSpec((rows, past), lambda i: (0, 0)),
        out_shape=jax.ShapeDtypeStruct((rows, past), F32),
        compiler_params=_params(("arbitrary",)),
        name="cache_gate_suffix",
    )(lf_t.reshape(rows, past)).reshape(bsz, nh, past)


def _fox_sample_kernel(q_ref, kn_ref, vn_ref, kc_ref, vc_ref, sfx_ref, col_ref, row_ref, nw_ref, o_ref, *, nh):
    t_len = q_ref.shape[0]
    past = kc_ref.shape[0]
    nw = nw_ref[...]
    heads = [slice(h * HD, (h + 1) * HD) for h in range(nh)]
    q_all = jnp.concatenate([q_ref[:, cols] for cols in heads], axis=0)
    cn_all = jnp.concatenate([col_ref[:, h:h + 1] for h in range(nh)], axis=0)
    k_flat = kc_ref[...].reshape(past * nh, HD).astype(BF16)
    v_flat = vc_ref[...].reshape(past * nh, HD).astype(BF16)
    rr = lax.broadcasted_iota(jnp.int32, (nh * t_len, past * nh), 0)
    cc = lax.broadcasted_iota(jnp.int32, (nh * t_len, past * nh), 1)
    own_head = (cc % nh) == (rr // t_len)
    s_c = jnp.where(own_head, _dot_nt(q_all, k_flat) + (cn_all + sfx_ref[...]), -jnp.inf)
    tr = lax.broadcasted_iota(jnp.int32, (t_len, t_len), 0)
    tc = lax.broadcasted_iota(jnp.int32, (t_len, t_len), 1)
    causal = tr >= tc
    s_n = []
    for h, cols in enumerate(heads):
        sn = _dot_nt(q_ref[:, cols], kn_ref[:, cols]) + (col_ref[:, h:h + 1] - row_ref[h:h + 1, :])
        s_n.append(jnp.where(causal, sn, -jnp.inf))
    m_n = jnp.concatenate([jnp.max(sn, axis=-1, keepdims=True) for sn in s_n], axis=0)
    m = jnp.maximum(jnp.max(s_c, axis=-1, keepdims=True), m_n)
    p_c = jnp.exp(s_c - m)
    l_c = jnp.sum(p_c, axis=-1, keepdims=True)
    o_c = _dot(p_c.astype(BF16), v_flat)
    for h, cols in enumerate(heads):
        rows = slice(h * t_len, (h + 1) * t_len)
        p_n = jnp.exp(s_n[h] - m[rows])
        l = l_c[rows] + jnp.sum(p_n, axis=-1, keepdims=True)
        o = (o_c[rows] + _dot(p_n.astype(BF16), vn_ref[:, cols])) / l
        o = o * lax.rsqrt(jnp.mean(o * o, axis=-1, keepdims=True) + RMS_EPS) * nw
        o_ref[:, cols] = o.astype(o_ref.dtype)


def fox_sample(qb, knb, vnb, k_cache, v_cache, layer, suffix_flat, col, row, norm_w, bsz, t_len, nh):
    w = nh * HD
    past = k_cache.shape[2]
    kern = functools.partial(_fox_sample_kernel, nh=nh)
    return pl.pallas_call(
        kern,
        grid=(bsz,),
        in_specs=[pl.BlockSpec((t_len, w), lambda b: (b, 0)),
                  pl.BlockSpec((t_len, w), lambda b: (b, 0)),
                  pl.BlockSpec((t_len, w), lambda b: (b, 0)),
                  pl.BlockSpec((None, None, past, nh, HD), lambda b: (layer, b, 0, 0, 0)),
                  pl.BlockSpec((None, None, past, nh, HD), lambda b: (layer, b, 0, 0, 0)),
                  pl.BlockSpec((None, 1, past * nh), lambda b: (b, 0, 0)),
                  pl.BlockSpec((None, t_len, LANES), lambda b: (b, 0, 0)),
                  pl.BlockSpec((None, 32, t_len), lambda b: (b, 0, 0)),
                  pl.BlockSpec((1, HD), lambda b: (0, 0))],
        out_specs=pl.BlockSpec((t_len, w), lambda b: (b, 0)),
        out_shape=jax.ShapeDtypeStruct((bsz * t_len, w), BF16),
        compiler_params=_params(("parallel",)),
        name="fox_sample",
    )(qb, knb, vnb, k_cache, v_cache, suffix_flat, col, row, norm_w)


def _gdn_prep_kernel(raw_ref, prev_ref, cbuf_ref, cw_ref, col_ref, row_ref,
                     u_ref, w_ref, qg_ref, kd_ref, qk_ref, ext_ref, *, nh, lc):
    t = pl.program_id(1)
    tb = raw_ref.shape[0]
    width = nh * HD
    pad = 8
    @pl.when(t == 0)
    def _():
        ext_ref[0:pad, :] = cbuf_ref[...]

    @pl.when(t > 0)
    def _():
        ext_ref[0:pad, :] = prev_ref[...]

    ext_ref[pad:pad + tb, :] = raw_ref[...]

    r = lax.broadcasted_iota(jnp.int32, (lc, lc), 0)
    c = lax.broadcasted_iota(jnp.int32, (lc, lc), 1)
    eye = (r == c).astype(F32)
    tril_f = (r >= c).astype(F32)
    offdiag_f = 1.0 - eye
    n_sq = int(math.log2(lc)) - 1
    qscale = HD ** -0.5
    zpad = jnp.zeros((lc, HD - lc), BF16) if lc < HD else None

    def conv_act(seg, h):
        lo = seg * width + h * HD
        acc = None
        for i in reversed(range(CONV_W)):
            term = ext_ref[pl.ds(pad - (CONV_W - 1) + i, tb), lo:lo + HD] * cw_ref[i:i + 1, lo:lo + HD]
            acc = term if acc is None else acc + term
        return _silu(acc)

    def l2n(x):
        return x * lax.rsqrt(jnp.sum(x * x, axis=-1, keepdims=True) + L2_EPS)

    units = []
    for h in range(nh):
        cols = slice(h * HD, (h + 1) * HD)
        q_all = l2n(conv_act(0, h)) * qscale
        k_all = l2n(conv_act(1, h))
        v_all = conv_act(2, h)
        for ci in range(tb // lc):
            rows = slice(ci * lc, (ci + 1) * lc)
            q = q_all[rows]
            k = k_all[rows]
            g_col = col_ref[rows, nh + h:nh + h + 1]
            b_col = col_ref[rows, 2 * nh + h:2 * nh + h + 1]
            g_row = row_ref[nh + h:nh + h + 1, rows]
            g_last = g_row[:, lc - 1:lc]
            kb = k * b_col
            qg_ref[rows, cols] = (q * jnp.exp(g_col)).astype(BF16)
            kd_ref[rows, cols] = (k * jnp.exp(g_last - g_col)).astype(BF16)
            units.append(dict(
                rows=rows, cols=cols,
                kbq=jnp.concatenate([kb, q], axis=0).astype(BF16),
                kbf=k.astype(BF16),
                decay=jnp.exp(jnp.minimum(g_col - g_row, 0.0)) * tril_f,
                rhs=jnp.concatenate([v_all[rows] * b_col, kb * jnp.exp(g_col)], axis=-1).astype(BF16)))

    for un in units:
        akq = _dot_nt(un["kbq"], un["kbf"])
        qkb = (akq[lc:] * un["decay"]).astype(BF16)
        if zpad is not None:
            qkb = jnp.concatenate([qkb, zpad], axis=-1)
        qk_ref[un["rows"], un["cols"]] = qkb
        un["p"] = -(akq[:lc] * (un["decay"] * offdiag_f))
        un["t"] = eye + un["p"]
    for un in units:
        pb = un["p"].astype(BF16)
        un["p"] = _dot(pb, pb)
    for _ in range(n_sq - 1):
        for un in units:
            st = _dot(jnp.concatenate([un["p"], un["t"]], axis=0).astype(BF16), un["p"].astype(BF16))
            un["p"] = st[:lc]
            un["t"] = un["t"] + st[lc:]
    for un in units:
        un["t"] = un["t"] + _dot(un["t"].astype(BF16), un["p"].astype(BF16))
    for un in units:
        uw = _dot(un["t"].astype(BF16), un["rhs"])
        u_ref[un["rows"], un["cols"]] = uw[:, :HD]
        w_ref[un["rows"], un["cols"]] = uw[:, HD:].astype(BF16)


def gdn_prep(raw, conv_state8, conv_w8, col, row, bsz, t_len, nh, lc):
    m = raw.shape[0]
    width = nh * HD
    c3 = 3 * width
    tb = _pick(t_len, (128, 64, 32, 16))
    nt = t_len // tb
    kern = functools.partial(_gdn_prep_kernel, nh=nh, lc=lc)
    blk = lambda b, t: (b * nt + t, 0)
    return pl.pallas_call(
        kern,
        grid=(bsz, nt),
        in_specs=[pl.BlockSpec((tb, c3), blk),
                  pl.BlockSpec((8, c3), lambda b, t: (jnp.maximum((b * nt + t) * (tb // 8) - 1, 0), 0)),
                  pl.BlockSpec((None, 8, c3), lambda b, t: (b, 0, 0)),
                  pl.BlockSpec((8, c3), lambda b, t: (0, 0)),
                  pl.BlockSpec((None, tb, LANES), lambda b, t: (b, t, 0)),
                  pl.BlockSpec((None, 32, tb), lambda b, t: (b, 0, t))],
        out_specs=[pl.BlockSpec((tb, width), blk) for _ in range(5)],
        out_shape=[jax.ShapeDtypeStruct((m, width), F32)] +
                  [jax.ShapeDtypeStruct((m, width), BF16) for _ in range(4)],
        scratch_shapes=[pltpu.VMEM((tb + 8, c3), F32)],
        compiler_params=_params(("parallel", "parallel")),
        name="gdn_prep",
    )(raw, raw, conv_state8, conv_w8, col, row)


def _gdn_scan_kernel(u_ref, w_ref, qg_ref, kd_ref, qk_ref, z_ref, row_ref, nw_ref, s0_ref,
                     o_ref, s_ref, *, nh, lc):
    t = pl.program_id(1)
    nb, tb = u_ref.shape[0], u_ref.shape[1]
    nw = nw_ref[...]

    @pl.when(t == 0)
    def _():
        s_ref[...] = s0_ref[...]

    units = [(p, h, slice(h * HD, (h + 1) * HD)) for p in range(nb) for h in range(nh)]
    states = [s_ref[p, h] for p, h, _ in units]
    for ci in range(tb // lc):
        rows = slice(ci * lc, (ci + 1) * lc)
        wqs = [_dot(jnp.concatenate([w_ref[p, rows, cols], qg_ref[p, rows, cols]], axis=0), states[n].astype(BF16))
               for n, (p, h, cols) in enumerate(units)]
        vbs = [(u_ref[p, rows, cols] - wqs[n][:lc]).astype(BF16) for n, (p, h, cols) in enumerate(units)]
        o2s = [_dot(qk_ref[p, rows, h * HD:h * HD + lc], vbs[n]) for n, (p, h, cols) in enumerate(units)]
        sds = [_dot_tn(kd_ref[p, rows, cols], vbs[n]) for n, (p, h, cols) in enumerate(units)]
        for n, (p, h, cols) in enumerate(units):
            g_last = row_ref[p, nh + h:nh + h + 1, ci * lc + lc - 1:ci * lc + lc]
            states[n] = states[n] * jnp.exp(g_last) + sds[n]
            o = wqs[n][lc:] + o2s[n]
            o = o * lax.rsqrt(jnp.mean(o * o, axis=-1, keepdims=True) + RMS_EPS) * nw * _silu(z_ref[p, rows, cols])
            o_ref[p, rows, cols] = o.astype(o_ref.dtype)
    for n, (p, h, _) in enumerate(units):
        s_ref[p, h] = states[n]


def gdn_scan(u, w, qg, kd, qk, z, row, norm_w, s0, bsz, t_len, nh, lc):
    m, width = u.shape
    tb = _pick(t_len, (256, 128, 64, 32, 16))
    nt = t_len // tb
    nb = 4 if bsz % 4 == 0 else (2 if bsz % 2 == 0 else 1)
    kern = functools.partial(_gdn_scan_kernel, nh=nh, lc=lc)
    rows3 = lambda a: a.reshape(bsz, t_len, width)
    blk = pl.BlockSpec((nb, tb, width), lambda b, t: (b, t, 0))
    mix, s_fin = pl.pallas_call(
        kern,
        grid=(bsz // nb, nt),
        in_specs=[blk for _ in range(6)] +
                 [pl.BlockSpec((nb, 32, tb), lambda b, t: (b, 0, t)),
                  pl.BlockSpec((1, HD), lambda b, t: (0, 0)),
                  pl.BlockSpec((nb, nh, HD, HD), lambda b, t: (b, 0, 0, 0))],
        out_specs=[blk, pl.BlockSpec((nb, nh, HD, HD), lambda b, t: (b, 0, 0, 0))],
        out_shape=[jax.ShapeDtypeStruct((bsz, t_len, width), BF16),
                   jax.ShapeDtypeStruct((bsz, nh, HD, HD), F32)],
        compiler_params=_params(("parallel", "arbitrary")),
        name="gdn_scan",
    )(rows3(u), rows3(w), rows3(qg), rows3(kd), rows3(qk), rows3(z), row, norm_w, s0)
    return mix.reshape(m, width), s_fin


def _out_ln_kernel(a_ref, b_ref, w_ref, res_ref, g_ref, beta_ref, o_ref, ob_ref, *, alpha):
    ka = a_ref.shape[1]
    y = _dot(a_ref[...], w_ref[0:ka, :]) + _dot(b_ref[...], w_ref[ka:, :])
    y = _ln_rows(alpha * res_ref[...] + y, g_ref[...], beta_ref[...])
    o_ref[...] = y
    ob_ref[...] = y.astype(BF16)


def out_proj_ln(mix_a, mix_b, w, res, g, beta, alpha):
    m, ka = mix_a.shape
    kb = mix_b.shape[1]
    d = w.shape[1]
    tm = _pick(m, (512, 256, 128, 64, 32, 16, 8))
    kern = functools.partial(_out_ln_kernel, alpha=alpha)
    row = lambda i: (i, 0)
    fixed = lambda i: (0, 0)
    return pl.pallas_call(
        kern,
        grid=(m // tm,),
        in_specs=[pl.BlockSpec((tm, ka), row), pl.BlockSpec((tm, kb), row),
                  pl.BlockSpec((ka + kb, d), fixed, pipeline_mode=pl.Buffered(1)),
                  pl.BlockSpec((tm, d), row),
                  pl.BlockSpec((1, d), fixed), pl.BlockSpec((1, d), fixed)],
        out_specs=[pl.BlockSpec((tm, d), row), pl.BlockSpec((tm, d), row)],
        out_shape=[jax.ShapeDtypeStruct((m, d), F32), jax.ShapeDtypeStruct((m, d), BF16)],
        compiler_params=_params(("parallel",)),
        name="out_proj_ln",
    )(mix_a, mix_b, w, res, g.reshape(1, d), beta.reshape(1, d))


def _ffn_up_kernel(*refs, cast_down):
    if cast_down:
        x_ref, wg_ref, wu_ref, wd_ref, h_ref, wdb_ref, wgb_ref, wub_ref = refs
    else:
        x_ref, wg_ref, wu_ref, h_ref, wgb_ref, wub_ref = refs

    @pl.when(pl.program_id(1) == 0)
    def _():
        for r0 in range(0, wg_ref.shape[0], CAST_ROWS):
            rows = slice(r0, r0 + CAST_ROWS)
            wgb_ref[rows, :] = wg_ref[rows, :].astype(BF16)
            wub_ref[rows, :] = wu_ref[rows, :].astype(BF16)
        if cast_down:
            wdb_ref[...] = wd_ref[...].astype(BF16)

    x = x_ref[...]
    gate = _dot(x, wgb_ref[...])
    up = _dot(x, wub_ref[...])
    h_ref[...] = (_silu(gate) * up).astype(h_ref.dtype)


def ffn_up(xb, wg, wu, layer, wd=None):
    m, d = xb.shape
    f = wg.shape[2]
    assert d % CAST_ROWS == 0
    tm = _pick(m, (1024, 512, 256, 128, 64, 32, 16, 8))
    tf = _pick(f, (512, 256, 128))
    wspec = pl.BlockSpec((None, d, tf), lambda j, i: (layer, 0, j))
    in_specs = [pl.BlockSpec((tm, d), lambda j, i: (i, 0)), wspec, wspec]
    out_specs = [pl.BlockSpec((tm, tf), lambda j, i: (i, j))]
    out_shape = [jax.ShapeDtypeStruct((m, f), BF16)]
    args = [xb, wg, wu]
    if wd is not None:
        dout = wd.shape[2]
        in_specs.append(pl.BlockSpec((None, tf, dout), lambda j, i: (layer, j, 0)))
        out_specs.append(pl.BlockSpec((tf, dout), lambda j, i: (j, 0)))
        out_shape.append(jax.ShapeDtypeStruct((f, dout), BF16))
        args.append(wd)
    return pl.pallas_call(
        functools.partial(_ffn_up_kernel, cast_down=wd is not None),
        grid=(f // tf, m // tm),
        in_specs=in_specs,
        out_specs=out_specs,
        out_shape=out_shape,
        scratch_shapes=[pltpu.VMEM((d, tf), BF16), pltpu.VMEM((d, tf), BF16)],
        compiler_params=_params(("parallel", "arbitrary")),
        name="ffn_up",
    )(*args)


def _ffn_down_kernel(h_ref, w_ref, res_ref, g_ref, beta_ref, o_ref, ob_ref, *, alpha):
    y = _dot(h_ref[...], w_ref[...])
    y = _ln_rows(alpha * res_ref[...] + y, g_ref[...], beta_ref[...])
    o_ref[...] = y
    ob_ref[...] = y.astype(BF16)


def ffn_down_ln(h, w, res, g, beta, alpha):
    m, f = h.shape
    d = w.shape[1]
    tm = _pick(m, (256, 128, 64, 32, 16, 8))
    kern = functools.partial(_ffn_down_kernel, alpha=alpha)
    row = lambda i: (i, 0)
    fixed = lambda i: (0, 0)
    return pl.pallas_call(
        kern,
        grid=(m // tm,),
        in_specs=[pl.BlockSpec((tm, f), row),
                  pl.BlockSpec((f, d), fixed, pipeline_mode=pl.Buffered(1)),
                  pl.BlockSpec((tm, d), row),
                  pl.BlockSpec((1, d), fixed), pl.BlockSpec((1, d), fixed)],
        out_specs=[pl.BlockSpec((tm, d), row), pl.BlockSpec((tm, d), row)],
        out_shape=[jax.ShapeDtypeStruct((m, d), F32), jax.ShapeDtypeStruct((m, d), BF16)],
        compiler_params=_params(("parallel",)),
        name="ffn_down_ln",
    )(h, w, res, g.reshape(1, d), beta.reshape(1, d))


def _layer(x, xb, bsz, t_len, layer, wts, lw, fox_cache, conv_state, s0, kv_prev, alpha):
    nh = lw["nh"]
    width = nh * HD
    lc = min(t_len, CHUNK)
    w_in = wts["w_in"]
    k_prev, v_prev = kv_prev if kv_prev is not None else (None, None)
    qb, = proj_segments(xb, w_in, layer, 0, 1, 0, (BF16,), out_scale=HD ** -0.5)
    kb, k_stack = proj_segments(xb, w_in, layer, 1, 1, 0, (BF16,), prev=k_prev, stack=True)
    vb, v_stack = proj_segments(xb, w_in, layer, 2, 1, 0, (BF16,), prev=v_prev, stack=True)
    raw, = proj_segments(xb, w_in, layer, 3, 3, nh, (F32,))
    z, = proj_segments(xb, w_in, layer, 6, 1, 3 * nh, (F32,))
    logf, col, row, qa, ka = gates(xb, w_in, layer, lw["gate_bias"], lw["alog_row"], bsz, t_len, lc, nh)

    if fox_cache is None:
        mix_f = fox_prompt(qb, qa, kb, ka, vb, lw["fox_norm_w"], bsz, t_len, nh)
    else:
        k_cache, v_cache, lf_cache = fox_cache
        sfx = cache_gate_suffix(jnp.transpose(lf_cache, (0, 2, 1)))
        sfx_flat = jnp.transpose(sfx, (0, 2, 1)).reshape(bsz, 1, -1)
        mix_f = fox_sample(qb, kb, vb, k_cache, v_cache, layer, sfx_flat, col, row,
                           lw["fox_norm_w"], bsz, t_len, nh)

    conv_state8 = jnp.pad(conv_state, ((0, 0), (8 - (CONV_W - 1), 0), (0, 0)))
    u, w, qg, kd, qk = gdn_prep(raw, conv_state8, lw["conv_w8"], col, row, bsz, t_len, nh, lc)
    mix_g, s_fin = gdn_scan(u, w, qg, kd, qk, z, row, lw["gdn_norm_w"], s0, bsz, t_len, nh, lc)

    x1, x1b = out_proj_ln(mix_f, mix_g, lw["w_out"], x, lw["ln_mix_g"], lw["ln_mix_b"], alpha)
    if "w_down" in lw:
        hmid, = ffn_up(x1b, wts["w_gate"], wts["w_up"], layer)
    else:
        hmid, lw["w_down"] = ffn_up(x1b, wts["w_gate"], wts["w_up"], layer, wd=wts["w_down"])
    x2, x2b = ffn_down_ln(hmid, lw["w_down"], x1, lw["ln_ffn_g"], lw["ln_ffn_b"], alpha)

    new_buf = raw.reshape(bsz, t_len, -1)[:, t_len - (CONV_W - 1):, :]
    return x2, x2b, (k_stack, v_stack), (logf, s_fin, new_buf)


def kernel(x_prompt, x_sample, cache_fox_k, cache_fox_v, cache_fox_logf, state_gdn, state_gdn_conv, ln_in_g, ln_in_b, w_in, fox_f_bias, fox_norm_w, gdn_conv_w, gdn_a_log, gdn_dt_bias, gdn_norm_w, w_out, ln_mix_g, ln_mix_b, ffn_w_gate, ffn_w_up, ffn_w_down, ln_ffn_g, ln_ffn_b):
    depth = w_in.shape[0]
    bp, tp, d = x_prompt.shape
    bs, ts, _ = x_sample.shape
    nh = fox_f_bias.shape[1]
    assert fox_norm_w.shape[1] == HD and gdn_norm_w.shape[1] == HD and gdn_a_log.shape[1] == nh
    width = nh * HD
    assert width == SEG and w_in.shape[2] == 7 * width + 3 * nh and gdn_conv_w.shape[2] == 3 * width
    alpha = (2 * depth) ** 0.25

    xp, xpb = layer_norm_rows(x_prompt.reshape(bp * tp, d), ln_in_g, ln_in_b)
    xs, xsb = layer_norm_rows(x_sample.reshape(bs * ts, d), ln_in_g, ln_in_b)

    zero_conv = jnp.zeros((bp, CONV_W - 1, 3 * width), F32)
    zero_state = jnp.zeros((bp, nh, HD, HD), F32)
    k_cache, v_cache = cache_fox_k, cache_fox_v
    wts = dict(w_in=jnp.swapaxes(w_in, 1, 2), w_gate=ffn_w_gate, w_up=ffn_w_up, w_down=ffn_w_down)
    kv_p = kv_s = None
    ps, ss = [], []
    for l in range(depth):
        gate_bias = jnp.concatenate([fox_f_bias[l], gdn_dt_bias[l], jnp.zeros((LANES - 2 * nh,), F32)]).reshape(1, LANES)
        alog_row = jnp.concatenate([jnp.zeros((nh,), F32), gdn_a_log[l], jnp.zeros((LANES - 2 * nh,), F32)]).reshape(1, LANES)
        lw = dict(
            nh=nh,
            gate_bias=gate_bias, alog_row=alog_row,
            fox_norm_w=fox_norm_w[l].reshape(1, HD),
            gdn_norm_w=gdn_norm_w[l].reshape(1, HD),
            conv_w8=jnp.pad(gdn_conv_w[l], ((0, 8 - CONV_W), (0, 0))),
            w_out=cast_bf16(w_out, l),
            ln_mix_g=ln_mix_g[l], ln_mix_b=ln_mix_b[l],
            ln_ffn_g=ln_ffn_g[l], ln_ffn_b=ln_ffn_b[l],
        )
        xp, xpb, kv_p, st_p = _layer(xp, xpb, bp, tp, l, wts, lw, None, zero_conv, zero_state, kv_p, alpha)
        xs, xsb, kv_s, st_s = _layer(xs, xsb, bs, ts, l, wts, lw, (k_cache, v_cache, cache_fox_logf[l]),
                                     state_gdn_conv[l], state_gdn[l], kv_s, alpha)
        ps.append(st_p)
        ss.append(st_s)

    def stacked(states, i):
        return jnp.stack([st[i] for st in states], axis=0)

    def heads(stack, bsz, t_len):
        return stack.reshape(depth, bsz, t_len, nh, HD)

    return (xp.reshape(bp, tp, d), xs.reshape(bs, ts, d),
            heads(kv_p[0], bp, tp), heads(kv_p[1], bp, tp), stacked(ps, 0), stacked(ps, 1), stacked(ps, 2),
            heads(kv_s[0], bs, ts), heads(kv_s[1], bs, ts), stacked(ss, 0), stacked(ss, 1), stacked(ss, 2))
```

```python
import functools
import math

import jax
import jax.numpy as jnp
from jax import lax
from jax.experimental import pallas as pl
from jax.experimental.pallas import tpu as pltpu

LN_EPS = 1e-5
RMS_EPS = 1e-6
L2_EPS = 1e-6
CHUNK = 64
CONV_W = 4
HD = 128
LANES = 128
VMEM_LIMIT = 56 * 1024 * 1024

F32 = jnp.float32
BF16 = jnp.bfloat16


def _pick(dim, cands):
    for c in cands:
        if c <= dim and dim % c == 0:
            return c
    return dim


def _params(sem):
    return pltpu.CompilerParams(dimension_semantics=sem, vmem_limit_bytes=VMEM_LIMIT)


def _split3(x):
    hi = x.astype(BF16)
    r1 = x - hi.astype(F32)
    mid = r1.astype(BF16)
    lo = (r1 - mid.astype(F32)).astype(BF16)
    return hi, mid, lo


def _dot(a, b):
    return jnp.dot(a, b, preferred_element_type=F32)


def _dot_nt(a, b):
    return lax.dot_general(a, b, (((1,), (1,)), ((), ())), preferred_element_type=F32)


def _dot_tn(a, b):
    return lax.dot_general(a, b, (((0,), (0,)), ((), ())), preferred_element_type=F32)


def _ln_rows(y, g, b):
    mu = jnp.mean(y, axis=-1, keepdims=True)
    d = y - mu
    var = jnp.mean(d * d, axis=-1, keepdims=True)
    return d * lax.rsqrt(var + LN_EPS) * g + b


def _silu(x):
    return x / (1.0 + jnp.exp(-x))


def _ln_kernel(x_ref, g_ref, b_ref, o_ref, ob_ref):
    y = _ln_rows(x_ref[...], g_ref[...], b_ref[...])
    o_ref[...] = y
    ob_ref[...] = y.astype(BF16)


def layer_norm_rows(x, g, b):
    m, d = x.shape
    tm = _pick(m, (512, 256, 128, 64, 32, 16, 8))
    return pl.pallas_call(
        _ln_kernel,
        grid=(m // tm,),
        in_specs=[pl.BlockSpec((tm, d), lambda i: (i, 0)),
                  pl.BlockSpec((1, d), lambda i: (0, 0)),
                  pl.BlockSpec((1, d), lambda i: (0, 0))],
        out_specs=[pl.BlockSpec((tm, d), lambda i: (i, 0)),
                   pl.BlockSpec((tm, d), lambda i: (i, 0))],
        out_shape=[jax.ShapeDtypeStruct((m, d), F32), jax.ShapeDtypeStruct((m, d), BF16)],
        compiler_params=_params(("parallel",)),
        name="ln_in",
    )(x, g.reshape(1, d), b.reshape(1, d))


def _cast_kernel(w_ref, o_ref):
    o_ref[...] = w_ref[...].astype(o_ref.dtype)


def cast_bf16(w, layer):
    _, k, n = w.shape
    tk = _pick(k, (512, 256, 128, 64, 32, 16))
    return pl.pallas_call(
        _cast_kernel,
        grid=(k // tk,),
        in_specs=[pl.BlockSpec((None, tk, n), lambda i: (layer, i, 0))],
        out_specs=pl.BlockSpec((tk, n), lambda i: (i, 0)),
        out_shape=jax.ShapeDtypeStruct((k, n), BF16),
        compiler_params=_params(("parallel",)),
        name="cast_bf16",
    )(w)


HEAD_GROUP = 1
SEG = 1024
CAST_ROWS = 256


def _proj_kernel(*refs, shift, n_out, n_prev, stack, out_scale):
    it = iter(refs)
    x_ref = next(it)
    wm_ref = next(it)
    wn_ref = next(it) if shift else None
    prev_ref = next(it) if n_prev else None
    o_refs = [next(it) for _ in range(n_out)]
    stack_ref = next(it) if stack else None
    wb_ref = next(it)

    @pl.when(pl.program_id(1) == 0)
    def _():
        for c0 in range(0, wm_ref.shape[1], CAST_ROWS):
            cols = slice(c0, c0 + CAST_ROWS)
            wm = wm_ref[:, cols]
            if shift:
                wm = jnp.concatenate([wm[shift:, :], wn_ref[0:shift, cols]], axis=0)
            wb_ref[cols, :] = wm.T.astype(BF16)

    acc = _dot(x_ref[...], wb_ref[...])
    for o in o_refs:
        o[...] = (acc if out_scale is None else acc * out_scale).astype(o.dtype)
    if stack:
        for l in range(n_prev):
            stack_ref[l] = prev_ref[l]
        stack_ref[n_prev] = acc.reshape(acc.shape[0], SEG // HD, HD)


def proj_segments(xb, wt, layer, seg0, nseg, shift, out_dtypes, prev=None, stack=False, out_scale=None):
    m, k = xb.shape
    assert k % CAST_ROWS == 0 and not (stack and nseg != 1) and shift % 8 == 0
    nb = 8 if shift <= 8 else 32
    assert shift <= nb and SEG % nb == 0
    n_prev = 0 if prev is None else prev.shape[0]
    tm = _pick(m, (1024, 512, 256, 128, 64, 32, 16, 8))
    single = pl.Buffered(1) if nseg == 1 else None
    in_specs = [pl.BlockSpec((tm, k), lambda j, i: (i, 0)),
                pl.BlockSpec((None, SEG, k), lambda j, i: (layer, seg0 + j, 0), pipeline_mode=single)]
    args = [xb, wt]
    if shift:
        in_specs.append(pl.BlockSpec((None, nb, k), lambda j, i: (layer, (seg0 + j + 1) * (SEG // nb), 0),
                                     pipeline_mode=single))
        args.append(wt)
    if n_prev:
        in_specs.append(pl.BlockSpec((n_prev, tm, SEG // HD, HD), lambda j, i: (0, i, 0, 0)))
        args.append(prev)
    out_specs = [pl.BlockSpec((tm, SEG), lambda j, i: (i, j)) for _ in out_dtypes]
    out_shape = [jax.ShapeDtypeStruct((m, nseg * SEG), dt) for dt in out_dtypes]
    if stack:
        out_specs.append(pl.BlockSpec((n_prev + 1, tm, SEG // HD, HD), lambda j, i: (0, i, 0, 0)))
        out_shape.append(jax.ShapeDtypeStruct((n_prev + 1, m, SEG // HD, HD), F32))
    kern = functools.partial(_proj_kernel, shift=shift, n_out=len(out_dtypes), n_prev=n_prev, stack=stack,
                             out_scale=out_scale)
    return pl.pallas_call(
        kern,
        grid=(nseg, m // tm),
        in_specs=in_specs,
        out_specs=out_specs,
        out_shape=out_shape,
        scratch_shapes=[pltpu.VMEM((k, SEG), BF16)],
        compiler_params=_params(("parallel", "arbitrary")),
        name="proj",
    )(*args)


def _gates_kernel(x_ref, wf_ref, wa_ref, wb_ref, bias_ref, alog_ref, logf_ref, col_ref, row_ref, qa_ref, ka_ref,
                  carry_ref, *, sb, chunk, nh):
    t = pl.program_id(1)
    tb = x_ref.shape[0]

    @pl.when(t == 0)
    def _():
        carry_ref[...] = jnp.zeros_like(carry_ref)

    lane = lax.broadcasted_iota(jnp.int32, (1, LANES), 1)
    wt = jnp.concatenate([wf_ref[...], wa_ref[...], wb_ref[...],
                          jnp.zeros((LANES - 3 * nh, wf_ref.shape[1]), F32)], axis=0).astype(BF16)
    raw = _dot_nt(x_ref[...], wt) + bias_ref[...]
    sp_neg = jnp.maximum(-raw, 0.0) + jnp.log1p(jnp.exp(-jnp.abs(raw)))
    sp_pos = jnp.maximum(raw, 0.0) + jnp.log1p(jnp.exp(-jnp.abs(raw)))
    logf = -sp_neg
    g = -jnp.exp(alog_ref[...]) * sp_pos
    beta = 1.0 / (1.0 + jnp.exp(-raw))
    is_f = lane < nh
    is_g = (lane >= nh) & (lane < 2 * nh)
    is_b = (lane >= 2 * nh) & (lane < 3 * nh)
    logf_ref[...] = logf[:, :nh]

    r = lax.broadcasted_iota(jnp.int32, (sb, sb), 0)
    c = lax.broadcasted_iota(jnp.int32, (sb, sb), 1)
    tri_full = (r >= c).astype(BF16)
    tri_chunk = ((r >= c) & ((r // chunk) == (c // chunk))).astype(BF16)
    er = lax.broadcasted_iota(jnp.int32, (32, LANES), 0)
    ec = lax.broadcasted_iota(jnp.int32, (32, LANES), 1)
    sel = (er == ec).astype(BF16)
    wide = nh * HD
    gr = lax.broadcasted_iota(jnp.int32, (LANES, wide), 0)
    gc = lax.broadcasted_iota(jnp.int32, (LANES, wide), 1)
    gh, gj = gr % nh, gr // nh
    spread_q = ((gc == gh * HD + 3 + gj) & (gj < 3)).astype(BF16)
    spread_k = ((gc == gh * HD + gj) & (gj < 3)).astype(BF16)
    lane_in_head = lax.broadcasted_iota(jnp.int32, (1, wide), 1) % HD
    ones_q = (lane_in_head < 3).astype(F32)
    ones_k = ((lane_in_head >= 3) & (lane_in_head < 6)).astype(F32)

    nsb = tb // sb
    blocks = [slice(s * sb, (s + 1) * sb) for s in range(nsb)]
    local = []
    for rows in blocks:
        cs = jnp.zeros((sb, LANES), F32)
        for part in _split3(jnp.where(is_f, logf[rows], 0.0)):
            cs = cs + _dot(tri_full, part)
        for part in _split3(jnp.where(is_g, g[rows], 0.0)):
            cs = cs + _dot(tri_chunk, part)
        local.append(cs)
    carry = carry_ref[...]
    colvs = []
    for rows, cs in zip(blocks, local):
        cs = cs + carry
        carry = jnp.where(is_f, cs[sb - 1:sb, :], 0.0)
        colv = jnp.where(is_b, beta[rows], cs)
        col_ref[rows, :] = colv
        colvs.append(colv)
    for rows, colv in zip(blocks, colvs):
        rv = jnp.zeros((32, sb), F32)
        terms = jnp.zeros((sb, LANES), F32)
        for j, part in enumerate(_split3(colv)):
            rv = rv + _dot_nt(sel, part)
            pf = jnp.where(is_f, part.astype(F32), 0.0)
            terms = terms + (pf if j == 0 else pltpu.roll(pf, j * nh, 1))
        tb16 = terms.astype(BF16)
        row_ref[:, rows] = rv
        qa_ref[rows, :] = (ones_q + _dot(tb16, spread_q)).astype(BF16)
        ka_ref[rows, :] = (ones_k - _dot(tb16, spread_k)).astype(BF16)
    carry_ref[...] = carry


def gates(xb, wt, layer, bias_row, alog_row, bsz, t_len, chunk, nh):
    m, d = xb.shape
    tb = _pick(t_len, (512, 256, 128, 64, 32, 16))
    sb = min(tb, 128)
    nt = t_len // tb
    width = nh * HD
    assert nh == 8 and 3 * nh <= LANES
    f_blk = 3 * width // nh
    a_blk = (6 * width + nh) // nh
    kern = functools.partial(_gates_kernel, sb=sb, chunk=chunk, nh=nh)
    return pl.pallas_call(
        kern,
        grid=(bsz, nt),
        in_specs=[pl.BlockSpec((tb, d), lambda b, t: (b * nt + t, 0)),
                  pl.BlockSpec((None, nh, d), lambda b, t: (layer, f_blk, 0)),
                  pl.BlockSpec((None, nh, d), lambda b, t: (layer, a_blk, 0)),
                  pl.BlockSpec((None, nh, d), lambda b, t: (layer, a_blk + 1, 0)),
                  pl.BlockSpec((1, LANES), lambda b, t: (0, 0)),
                  pl.BlockSpec((1, LANES), lambda b, t: (0, 0))],
        out_specs=[pl.BlockSpec((None, tb, nh), lambda b, t: (b, t, 0)),
                   pl.BlockSpec((None, tb, LANES), lambda b, t: (b, t, 0)),
                   pl.BlockSpec((None, 32, tb), lambda b, t: (b, 0, t)),
                   pl.BlockSpec((tb, width), lambda b, t: (b * nt + t, 0)),
                   pl.BlockSpec((tb, width), lambda b, t: (b * nt + t, 0))],
        out_shape=[jax.ShapeDtypeStruct((bsz, t_len, nh), F32),
                   jax.ShapeDtypeStruct((bsz, t_len, LANES), F32),
                   jax.ShapeDtypeStruct((bsz, 32, t_len), F32),
                   jax.ShapeDtypeStruct((m, width), BF16),
                   jax.ShapeDtypeStruct((m, width), BF16)],
        scratch_shapes=[pltpu.VMEM((1, LANES), F32)],
        compiler_params=_params(("parallel", "arbitrary")),
        name="gates",
    )(xb, wt, wt, wt, bias_row, alog_row)


def _fox_prompt_kernel(q_ref, qa_ref, k_ref, ka_ref, v_ref, nw_ref, o_ref, q2_ref, m_ref, acc_ref, *, nh, tq):
    qi = pl.program_id(1)
    nw = nw_ref[...]
    r = lax.broadcasted_iota(jnp.int32, (tq, LANES), 0)
    c = lax.broadcasted_iota(jnp.int32, (tq, LANES), 1)
    heads = [slice(h * HD, (h + 1) * HD) for h in range(nh)]
    for h, cols in enumerate(heads):
        q2_ref[h] = jnp.concatenate([q_ref[:, cols], qa_ref[:, cols]], axis=-1)

    def scores(start, width, group, masked):
        out = []
        for h in group:
            cols = heads[h]
            k2 = jnp.concatenate([k_ref[pl.ds(start, width), cols], ka_ref[pl.ds(start, width), cols]], axis=-1)
            s = _dot_nt(q2_ref[h], k2)
            pieces = [s[:, k * LANES:(k + 1) * LANES] for k in range(width // LANES)]
            if masked:
                pieces = [jnp.where(r >= c + k * LANES, sk, -jnp.inf) for k, sk in enumerate(pieces)]
            out.append(pieces)
        return out

    def row_max(pieces):
        mx = pieces[0]
        for sk in pieces[1:]:
            mx = jnp.maximum(mx, sk)
        return jnp.broadcast_to(jnp.max(mx, axis=-1, keepdims=True), (tq, LANES))

    def pv(p_pieces, start, width, h):
        p = jnp.concatenate(p_pieces, axis=-1).astype(BF16)
        v_aug = jnp.concatenate([v_ref[pl.ds(start, width), heads[h]], jnp.ones((width, LANES), BF16)], axis=-1)
        return _dot(p, v_aug)

    groups = [list(range(g, min(g + HEAD_GROUP, nh))) for g in range(0, nh, HEAD_GROUP)]

    def block(start, width, first):
        pending = scores(start, width, groups[0], first)
        for gi, group in enumerate(groups):
            ss = pending
            if gi + 1 < len(groups):
                pending = scores(start, width, groups[gi + 1], first)
            ps, alphas = [], []
            for h, pieces in zip(group, ss):
                mx = row_max(pieces)
                if first:
                    m_new = mx
                else:
                    m = m_ref[h]
                    m_new = jnp.maximum(m, mx)
                    alphas.append(jnp.exp(m - m_new))
                m_ref[h] = m_new
                ps.append([jnp.exp(sk - m_new) for sk in pieces])
            for i, (h, pp) in enumerate(zip(group, ps)):
                if first:
                    acc_ref[h] = pv(pp, start, width, h)
                else:
                    a = alphas[i]
                    acc_ref[h] = jnp.concatenate([a, a], axis=-1) * acc_ref[h] + pv(pp, start, width, h)

    block(pl.multiple_of(qi * tq, tq), tq, True)

    @pl.when(qi % 2 == 1)
    def _():
        block(pl.multiple_of((qi - 1) * tq, tq), tq, False)

    def step(j, carry):
        block(pl.multiple_of(j * (2 * tq), 2 * tq), 2 * tq, False)
        return carry

    lax.fori_loop(0, qi // 2, step, 0)
    for h in range(nh):
        acc = acc_ref[h]
        o = acc[:, :HD] / acc[:, HD:]
        o = o * lax.rsqrt(jnp.mean(o * o, axis=-1, keepdims=True) + RMS_EPS) * nw
        o_ref[:, heads[h]] = o.astype(o_ref.dtype)


def fox_prompt(qb, qa, kb, ka, vb, norm_w, bsz, t_len, nh):
    tq = _pick(t_len, (512, 256, 128))
    assert tq % LANES == 0 and t_len % tq == 0
    nq = t_len // tq
    w = nh * HD
    kern = functools.partial(_fox_prompt_kernel, nh=nh, tq=tq)
    qspec = pl.BlockSpec((tq, w), lambda b, i: (b * nq + i, 0))
    kspec = pl.BlockSpec((t_len, w), lambda b, i: (b, 0))
    return pl.pallas_call(
        kern,
        grid=(bsz, nq),
        in_specs=[qspec, qspec, kspec, kspec, kspec, pl.BlockSpec((1, HD), lambda b, i: (0, 0))],
        out_specs=pl.BlockSpec((tq, w), lambda b, i: (b * nq + i, 0)),
        out_shape=jax.ShapeDtypeStruct((bsz * t_len, w), BF16),
        scratch_shapes=[pltpu.VMEM((nh, tq, 2 * HD), BF16), pltpu.VMEM((nh, tq, LANES), F32),
                        pltpu.VMEM((nh, tq, 2 * HD), F32)],
        compiler_params=_params(("parallel", "parallel")),
        name="fox_prompt",
    )(qb, qa, kb, ka, vb, norm_w)


def _cache_gate_kernel(lf_ref, o_ref):
    past = lf_ref.shape[1]
    r = lax.broadcasted_iota(jnp.int32, (past, past), 0)
    c = lax.broadcasted_iota(jnp.int32, (past, past), 1)
    later = (r > c).astype(BF16)
    acc = jnp.zeros(lf_ref.shape, F32)
    for part in _split3(lf_ref[...]):
        acc = acc + _dot(part, later)
    o_ref[...] = acc


def cache_gate_suffix(lf_t):
    bsz, nh, past = lf_t.shape
    rows = bsz * nh
    return pl.pallas_call(
        _cache_gate_kernel,
        grid=(1,),
        in_specs=[pl.BlockSpec((rows, past), lambda i: (0, 0))],
        out_specs=pl.BlockSpec((rows, past), lambda i: (0, 0)),
        out_shape=jax.ShapeDtypeStruct((rows, past), F32),
        compiler_params=_params(("arbitrary",)),
        name="cache_gate_suffix",
    )(lf_t.reshape(rows, past)).reshape(bsz, nh, past)


def _fox_sample_kernel(q_ref, kn_ref, vn_ref, kc_ref, vc_ref, sfx_ref, col_ref, row_ref, nw_ref, o_ref, *, nh):
    t_len = q_ref.shape[0]
    past = kc_ref.shape[0]
    nw = nw_ref[...]
    heads = [slice(h * HD, (h + 1) * HD) for h in range(nh)]
    q_all = jnp.concatenate([q_ref[:, cols] for cols in heads], axis=0)
    cn_all = jnp.concatenate([col_ref[:, h:h + 1] for h in range(nh)], axis=0)
    k_flat = kc_ref[...].reshape(past * nh, HD).astype(BF16)
    v_flat = vc_ref[...].reshape(past * nh, HD).astype(BF16)
    rr = lax.broadcasted_iota(jnp.int32, (nh * t_len, past * nh), 0)
    cc = lax.broadcasted_iota(jnp.int32, (nh * t_len, past * nh), 1)
    own_head = (cc % nh) == (rr // t_len)
    s_c = jnp.where(own_head, _dot_nt(q_all, k_flat) + (cn_all + sfx_ref[...]), -jnp.inf)
    tr = lax.broadcasted_iota(jnp.int32, (t_len, t_len), 0)
    tc = lax.broadcasted_iota(jnp.int32, (t_len, t_len), 1)
    causal = tr >= tc
    s_n = []
    for h, cols in enumerate(heads):
        sn = _dot_nt(q_ref[:, cols], kn_ref[:, cols]) + (col_ref[:, h:h + 1] - row_ref[h:h + 1, :])
        s_n.append(jnp.where(causal, sn, -jnp.inf))
    m_n = jnp.concatenate([jnp.max(sn, axis=-1, keepdims=True) for sn in s_n], axis=0)
    m = jnp.maximum(jnp.max(s_c, axis=-1, keepdims=True), m_n)
    p_c = jnp.exp(s_c - m)
    l_c = jnp.sum(p_c, axis=-1, keepdims=True)
    o_c = _dot(p_c.astype(BF16), v_flat)
    for h, cols in enumerate(heads):
        rows = slice(h * t_len, (h + 1) * t_len)
        p_n = jnp.exp(s_n[h] - m[rows])
        l = l_c[rows] + jnp.sum(p_n, axis=-1, keepdims=True)
        o = (o_c[rows] + _dot(p_n.astype(BF16), vn_ref[:, cols])) / l
        o = o * lax.rsqrt(jnp.mean(o * o, axis=-1, keepdims=True) + RMS_EPS) * nw
        o_ref[:, cols] = o.astype(o_ref.dtype)


def fox_sample(qb, knb, vnb, k_cache, v_cache, layer, suffix_flat, col, row, norm_w, bsz, t_len, nh):
    w = nh * HD
    past = k_cache.shape[2]
    kern = functools.partial(_fox_sample_kernel, nh=nh)
    return pl.pallas_call(
        kern,
        grid=(bsz,),
        in_specs=[pl.BlockSpec((t_len, w), lambda b: (b, 0)),
                  pl.BlockSpec((t_len, w), lambda b: (b, 0)),
                  pl.BlockSpec((t_len, w), lambda b: (b, 0)),
                  pl.BlockSpec((None, None, past, nh, HD), lambda b: (layer, b, 0, 0, 0)),
                  pl.BlockSpec((None, None, past, nh, HD), lambda b: (layer, b, 0, 0, 0)),
                  pl.BlockSpec((None, 1, past * nh), lambda b: (b, 0, 0)),
                  pl.BlockSpec((None, t_len, LANES), lambda b: (b, 0, 0)),
                  pl.BlockSpec((None, 32, t_len), lambda b: (b, 0, 0)),
                  pl.BlockSpec((1, HD), lambda b: (0, 0))],
        out_specs=pl.BlockSpec((t_len, w), lambda b: (b, 0)),
        out_shape=jax.ShapeDtypeStruct((bsz * t_len, w), BF16),
        compiler_params=_params(("parallel",)),
        name="fox_sample",
    )(qb, knb, vnb, k_cache, v_cache, suffix_flat, col, row, norm_w)


def _gdn_prep_kernel(raw_ref, prev_ref, cbuf_ref, cw_ref, col_ref, row_ref,
                     u_ref, w_ref, qg_ref, kd_ref, qk_ref, ext_ref, *, nh, lc):
    t = pl.program_id(1)
    tb = raw_ref.shape[0]
    width = nh * HD
    pad = 8
    @pl.when(t == 0)
    def _():
        ext_ref[0:pad, :] = cbuf_ref[...]

    @pl.when(t > 0)
    def _():
        ext_ref[0:pad, :] = prev_ref[...]

    ext_ref[pad:pad + tb, :] = raw_ref[...]

    r = lax.broadcasted_iota(jnp.int32, (lc, lc), 0)
    c = lax.broadcasted_iota(jnp.int32, (lc, lc), 1)
    eye = (r == c).astype(F32)
    tril_f = (r >= c).astype(F32)
    offdiag_f = 1.0 - eye
    n_sq = int(math.log2(lc)) - 1
    qscale = HD ** -0.5
    zpad = jnp.zeros((lc, HD - lc), BF16) if lc < HD else None

    def conv_act(seg, h):
        lo = seg * width + h * HD
        acc = None
        for i in reversed(range(CONV_W)):
            term = ext_ref[pl.ds(pad - (CONV_W - 1) + i, tb), lo:lo + HD] * cw_ref[i:i + 1, lo:lo + HD]
            acc = term if acc is None else acc + term
        return _silu(acc)

    def l2n(x):
        return x * lax.rsqrt(jnp.sum(x * x, axis=-1, keepdims=True) + L2_EPS)

    units = []
    for h in range(nh):
        cols = slice(h * HD, (h + 1) * HD)
        q_all = l2n(conv_act(0, h)) * qscale
        k_all = l2n(conv_act(1, h))
        v_all = conv_act(2, h)
        for ci in range(tb // lc):
            rows = slice(ci * lc, (ci + 1) * lc)
            q = q_all[rows]
            k = k_all[rows]
            g_col = col_ref[rows, nh + h:nh + h + 1]
            b_col = col_ref[rows, 2 * nh + h:2 * nh + h + 1]
            g_row = row_ref[nh + h:nh + h + 1, rows]
            g_last = g_row[:, lc - 1:lc]
            kb = k * b_col
            qg_ref[rows, cols] = (q * jnp.exp(g_col)).astype(BF16)
            kd_ref[rows, cols] = (k * jnp.exp(g_last - g_col)).astype(BF16)
            units.append(dict(
                rows=rows, cols=cols,
                kbq=jnp.concatenate([kb, q], axis=0).astype(BF16),
                kbf=k.astype(BF16),
                decay=jnp.exp(jnp.minimum(g_col - g_row, 0.0)) * tril_f,
                rhs=jnp.concatenate([v_all[rows] * b_col, kb * jnp.exp(g_col)], axis=-1).astype(BF16)))

    for un in units:
        akq = _dot_nt(un["kbq"], un["kbf"])
        qkb = (akq[lc:] * un["decay"]).astype(BF16)
        if zpad is not None:
            qkb = jnp.concatenate([qkb, zpad], axis=-1)
        qk_ref[un["rows"], un["cols"]] = qkb
        un["p"] = -(akq[:lc] * (un["decay"] * offdiag_f))
        un["t"] = eye + un["p"]
    for un in units:
        pb = un["p"].astype(BF16)
        un["p"] = _dot(pb, pb)
    for _ in range(n_sq - 1):
        for un in units:
            st = _dot(jnp.concatenate([un["p"], un["t"]], axis=0).astype(BF16), un["p"].astype(BF16))
            un["p"] = st[:lc]
            un["t"] = un["t"] + st[lc:]
    for un in units:
        un["t"] = un["t"] + _dot(un["t"].astype(BF16), un["p"].astype(BF16))
    for un in units:
        uw = _dot(un["t"].astype(BF16), un["rhs"])
        u_ref[un["rows"], un["cols"]] = uw[:, :HD]
        w_ref[un["rows"], un["cols"]] = uw[:, HD:].astype(BF16)


def gdn_prep(raw, conv_state8, conv_w8, col, row, bsz, t_len, nh, lc):
    m = raw.shape[0]
    width = nh * HD
    c3 = 3 * width
    tb = _pick(t_len, (128, 64, 32, 16))
    nt = t_len // tb
    kern = functools.partial(_gdn_prep_kernel, nh=nh, lc=lc)
    blk = lambda b, t: (b * nt + t, 0)
    return pl.pallas_call(
        kern,
        grid=(bsz, nt),
        in_specs=[pl.BlockSpec((tb, c3), blk),
                  pl.BlockSpec((8, c3), lambda b, t: (jnp.maximum((b * nt + t) * (tb // 8) - 1, 0), 0)),
                  pl.BlockSpec((None, 8, c3), lambda b, t: (b, 0, 0)),
                  pl.BlockSpec((8, c3), lambda b, t: (0, 0)),
                  pl.BlockSpec((None, tb, LANES), lambda b, t: (b, t, 0)),
                  pl.BlockSpec((None, 32, tb), lambda b, t: (b, 0, t))],
        out_specs=[pl.BlockSpec((tb, width), blk) for _ in range(5)],
        out_shape=[jax.ShapeDtypeStruct((m, width), F32)] +
                  [jax.ShapeDtypeStruct((m, width), BF16) for _ in range(4)],
        scratch_shapes=[pltpu.VMEM((tb + 8, c3), F32)],
        compiler_params=_params(("parallel", "parallel")),
        name="gdn_prep",
    )(raw, raw, conv_state8, conv_w8, col, row)


def _gdn_scan_kernel(u_ref, w_ref, qg_ref, kd_ref, qk_ref, z_ref, row_ref, nw_ref, s0_ref,
                     o_ref, s_ref, *, nh, lc):
    t = pl.program_id(1)
    nb, tb = u_ref.shape[0], u_ref.shape[1]
    nw = nw_ref[...]

    @pl.when(t == 0)
    def _():
        s_ref[...] = s0_ref[...]

    units = [(p, h, slice(h * HD, (h + 1) * HD)) for p in range(nb) for h in range(nh)]
    states = [s_ref[p, h] for p, h, _ in units]
    for ci in range(tb // lc):
        rows = slice(ci * lc, (ci + 1) * lc)
        wqs = [_dot(jnp.concatenate([w_ref[p, rows, cols], qg_ref[p, rows, cols]], axis=0), states[n].astype(BF16))
               for n, (p, h, cols) in enumerate(units)]
        vbs = [(u_ref[p, rows, cols] - wqs[n][:lc]).astype(BF16) for n, (p, h, cols) in enumerate(units)]
        o2s = [_dot(qk_ref[p, rows, h * HD:h * HD + lc], vbs[n]) for n, (p, h, cols) in enumerate(units)]
        sds = [_dot_tn(kd_ref[p, rows, cols], vbs[n]) for n, (p, h, cols) in enumerate(units)]
        for n, (p, h, cols) in enumerate(units):
            g_last = row_ref[p, nh + h:nh + h + 1, ci * lc + lc - 1:ci * lc + lc]
            states[n] = states[n] * jnp.exp(g_last) + sds[n]
            o = wqs[n][lc:] + o2s[n]
            o = o * lax.rsqrt(jnp.mean(o * o, axis=-1, keepdims=True) + RMS_EPS) * nw * _silu(z_ref[p, rows, cols])
            o_ref[p, rows, cols] = o.astype(o_ref.dtype)
    for n, (p, h, _) in enumerate(units):
        s_ref[p, h] = states[n]


def gdn_scan(u, w, qg, kd, qk, z, row, norm_w, s0, bsz, t_len, nh, lc):
    m, width = u.shape
    tb = _pick(t_len, (256, 128, 64, 32, 16))
    nt = t_len // tb
    nb = 4 if bsz % 4 == 0 else (2 if bsz % 2 == 0 else 1)
    kern = functools.partial(_gdn_scan_kernel, nh=nh, lc=lc)
    rows3 = lambda a: a.reshape(bsz, t_len, width)
    blk = pl.BlockSpec((nb, tb, width), lambda b, t: (b, t, 0))
    mix, s_fin = pl.pallas_call(
        kern,
        grid=(bsz // nb, nt),
        in_specs=[blk for _ in range(6)] +
                 [pl.BlockSpec((nb, 32, tb), lambda b, t: (b, 0, t)),
                  pl.BlockSpec((1, HD), lambda b, t: (0, 0)),
                  pl.BlockSpec((nb, nh, HD, HD), lambda b, t: (b, 0, 0, 0))],
        out_specs=[blk, pl.BlockSpec((nb, nh, HD, HD), lambda b, t: (b, 0, 0, 0))],
        out_shape=[jax.ShapeDtypeStruct((bsz, t_len, width), BF16),
                   jax.ShapeDtypeStruct((bsz, nh, HD, HD), F32)],
        compiler_params=_params(("parallel", "arbitrary")),
        name="gdn_scan",
    )(rows3(u), rows3(w), rows3(qg), rows3(kd), rows3(qk), rows3(z), row, norm_w, s0)
    return mix.reshape(m, width), s_fin


def _out_ln_kernel(a_ref, b_ref, w_ref, res_ref, g_ref, beta_ref, o_ref, ob_ref, *, alpha):
    ka = a_ref.shape[1]
    y = _dot(a_ref[...], w_ref[0:ka, :]) + _dot(b_ref[...], w_ref[ka:, :])
    y = _ln_rows(alpha * res_ref[...] + y, g_ref[...], beta_ref[...])
    o_ref[...] = y
    ob_ref[...] = y.astype(BF16)


def out_proj_ln(mix_a, mix_b, w, res, g, beta, alpha):
    m, ka = mix_a.shape
    kb = mix_b.shape[1]
    d = w.shape[1]
    tm = _pick(m, (512, 256, 128, 64, 32, 16, 8))
    kern = functools.partial(_out_ln_kernel, alpha=alpha)
    row = lambda i: (i, 0)
    fixed = lambda i: (0, 0)
    return pl.pallas_call(
        kern,
        grid=(m // tm,),
        in_specs=[pl.BlockSpec((tm, ka), row), pl.BlockSpec((tm, kb), row),
                  pl.BlockSpec((ka + kb, d), fixed, pipeline_mode=pl.Buffered(1)),
                  pl.BlockSpec((tm, d), row),
                  pl.BlockSpec((1, d), fixed), pl.BlockSpec((1, d), fixed)],
        out_specs=[pl.BlockSpec((tm, d), row), pl.BlockSpec((tm, d), row)],
        out_shape=[jax.ShapeDtypeStruct((m, d), F32), jax.ShapeDtypeStruct((m, d), BF16)],
        compiler_params=_params(("parallel",)),
        name="out_proj_ln",
    )(mix_a, mix_b, w, res, g.reshape(1, d), beta.reshape(1, d))


def _ffn_up_kernel(*refs, cast_down):
    if cast_down:
        x_ref, wg_ref, wu_ref, wd_ref, h_ref, wdb_ref, wgb_ref, wub_ref = refs
    else:
        x_ref, wg_ref, wu_ref, h_ref, wgb_ref, wub_ref = refs

    @pl.when(pl.program_id(1) == 0)
    def _():
        for r0 in range(0, wg_ref.shape[0], CAST_ROWS):
            rows = slice(r0, r0 + CAST_ROWS)
            wgb_ref[rows, :] = wg_ref[rows, :].astype(BF16)
            wub_ref[rows, :] = wu_ref[rows, :].astype(BF16)
        if cast_down:
            wdb_ref[...] = wd_ref[...].astype(BF16)

    x = x_ref[...]
    gate = _dot(x, wgb_ref[...])
    up = _dot(x, wub_ref[...])
    h_ref[...] = (_silu(gate) * up).astype(h_ref.dtype)


def ffn_up(xb, wg, wu, layer, wd=None):
    m, d = xb.shape
    f = wg.shape[2]
    assert d % CAST_ROWS == 0
    tm = _pick(m, (1024, 512, 256, 128, 64, 32, 16, 8))
    tf = _pick(f, (512, 256, 128))
    wspec = pl.BlockSpec((None, d, tf), lambda j, i: (layer, 0, j))
    in_specs = [pl.BlockSpec((tm, d), lambda j, i: (i, 0)), wspec, wspec]
    out_specs = [pl.BlockSpec((tm, tf), lambda j, i: (i, j))]
    out_shape = [jax.ShapeDtypeStruct((m, f), BF16)]
    args = [xb, wg, wu]
    if wd is not None:
        dout = wd.shape[2]
        in_specs.append(pl.BlockSpec((None, tf, dout), lambda j, i: (layer, j, 0)))
        out_specs.append(pl.BlockSpec((tf, dout), lambda j, i: (j, 0)))
        out_shape.append(jax.ShapeDtypeStruct((f, dout), BF16))
        args.append(wd)
    return pl.pallas_call(
        functools.partial(_ffn_up_kernel, cast_down=wd is not None),
        grid=(f // tf, m // tm),
        in_specs=in_specs,
        out_specs=out_specs,
        out_shape=out_shape,
        scratch_shapes=[pltpu.VMEM((d, tf), BF16), pltpu.VMEM((d, tf), BF16)],
        compiler_params=_params(("parallel", "arbitrary")),
        name="ffn_up",
    )(*args)


def _ffn_down_kernel(h_ref, w_ref, res_ref, g_ref, beta_ref, o_ref, ob_ref, *, alpha):
    y = _dot(h_ref[...], w_ref[...])
    y = _ln_rows(alpha * res_ref[...] + y, g_ref[...], beta_ref[...])
    o_ref[...] = y
    ob_ref[...] = y.astype(BF16)


def ffn_down_ln(h, w, res, g, beta, alpha):
    m, f = h.shape
    d = w.shape[1]
    tm = _pick(m, (256, 128, 64, 32, 16, 8))
    kern = functools.partial(_ffn_down_kernel, alpha=alpha)
    row = lambda i: (i, 0)
    fixed = lambda i: (0, 0)
    return pl.pallas_call(
        kern,
        grid=(m // tm,),
        in_specs=[pl.BlockSpec((tm, f), row),
                  pl.BlockSpec((f, d), fixed, pipeline_mode=pl.Buffered(1)),
                  pl.BlockSpec((tm, d), row),
                  pl.BlockSpec((1, d), fixed), pl.BlockSpec((1, d), fixed)],
        out_specs=[pl.BlockSpec((tm, d), row), pl.BlockSpec((tm, d), row)],
        out_shape=[jax.ShapeDtypeStruct((m, d), F32), jax.ShapeDtypeStruct((m, d), BF16)],
        compiler_params=_params(("parallel",)),
        name="ffn_down_ln",
    )(h, w, res, g.reshape(1, d), beta.reshape(1, d))


def _layer(x, xb, bsz, t_len, layer, wts, lw, fox_cache, conv_state, s0, kv_prev, alpha):
    nh = lw["nh"]
    width = nh * HD
    lc = min(t_len, CHUNK)
    w_in = wts["w_in"]
    k_prev, v_prev = kv_prev if kv_prev is not None else (None, None)
    qb, = proj_segments(xb, w_in, layer, 0, 1, 0, (BF16,), out_scale=HD ** -0.5)
    kb, k_stack = proj_segments(xb, w_in, layer, 1, 1, 0, (BF16,), prev=k_prev, stack=True)
    vb, v_stack = proj_segments(xb, w_in, layer, 2, 1, 0, (BF16,), prev=v_prev, stack=True)
    raw, = proj_segments(xb, w_in, layer, 3, 3, nh, (F32,))
    z, = proj_segments(xb, w_in, layer, 6, 1, 3 * nh, (F32,))
    logf, col, row, qa, ka = gates(xb, w_in, layer, lw["gate_bias"], lw["alog_row"], bsz, t_len, lc, nh)

    if fox_cache is None:
        mix_f = fox_prompt(qb, qa, kb, ka, vb, lw["fox_norm_w"], bsz, t_len, nh)
    else:
        k_cache, v_cache, lf_cache = fox_cache
        sfx = cache_gate_suffix(jnp.transpose(lf_cache, (0, 2, 1)))
        sfx_flat = jnp.transpose(sfx, (0, 2, 1)).reshape(bsz, 1, -1)
        mix_f = fox_sample(qb, kb, vb, k_cache, v_cache, layer, sfx_flat, col, row,
                           lw["fox_norm_w"], bsz, t_len, nh)

    conv_state8 = jnp.pad(conv_state, ((0, 0), (8 - (CONV_W - 1), 0), (0, 0)))
    u, w, qg, kd, qk = gdn_prep(raw, conv_state8, lw["conv_w8"], col, row, bsz, t_len, nh, lc)
    mix_g, s_fin = gdn_scan(u, w, qg, kd, qk, z, row, lw["gdn_norm_w"], s0, bsz, t_len, nh, lc)

    x1, x1b = out_proj_ln(mix_f, mix_g, lw["w_out"], x, lw["ln_mix_g"], lw["ln_mix_b"], alpha)
    if "w_down" in lw:
        hmid, = ffn_up(x1b, wts["w_gate"], wts["w_up"], layer)
    else:
        hmid, lw["w_down"] = ffn_up(x1b, wts["w_gate"], wts["w_up"], layer, wd=wts["w_down"])
    x2, x2b = ffn_down_ln(hmid, lw["w_down"], x1, lw["ln_ffn_g"], lw["ln_ffn_b"], alpha)

    new_buf = raw.reshape(bsz, t_len, -1)[:, t_len - (CONV_W - 1):, :]
    return x2, x2b, (k_stack, v_stack), (logf, s_fin, new_buf)


def kernel(x_prompt, x_sample, cache_fox_k, cache_fox_v, cache_fox_logf, state_gdn, state_gdn_conv, ln_in_g, ln_in_b, w_in, fox_f_bias, fox_norm_w, gdn_conv_w, gdn_a_log, gdn_dt_bias, gdn_norm_w, w_out, ln_mix_g, ln_mix_b, ffn_w_gate, ffn_w_up, ffn_w_down, ln_ffn_g, ln_ffn_b):
    depth = w_in.shape[0]
    bp, tp, d = x_prompt.shape
    bs, ts, _ = x_sample.shape
    nh = fox_f_bias.shape[1]
    assert fox_norm_w.shape[1] == HD and gdn_norm_w.shape[1] == HD and gdn_a_log.shape[1] == nh
    width = nh * HD
    assert width == SEG and w_in.shape[2] == 7 * width + 3 * nh and gdn_conv_w.shape[2] == 3 * width
    alpha = (2 * depth) ** 0.25

    xp, xpb = layer_norm_rows(x_prompt.reshape(bp * tp, d), ln_in_g, ln_in_b)
    xs, xsb = layer_norm_rows(x_sample.reshape(bs * ts, d), ln_in_g, ln_in_b)

    zero_conv = jnp.zeros((bp, CONV_W - 1, 3 * width), F32)
    zero_state = jnp.zeros((bp, nh, HD, HD), F32)
    k_cache, v_cache = cache_fox_k, cache_fox_v
    wts = dict(w_in=jnp.swapaxes(w_in, 1, 2), w_gate=ffn_w_gate, w_up=ffn_w_up, w_down=ffn_w_down)
    kv_p = kv_s = None
    ps, ss = [], []
    for l in range(depth):
        gate_bias = jnp.concatenate([fox_f_bias[l], gdn_dt_bias[l], jnp.zeros((LANES - 2 * nh,), F32)]).reshape(1, LANES)
        alog_row = jnp.concatenate([jnp.zeros((nh,), F32), gdn_a_log[l], jnp.zeros((LANES - 2 * nh,), F32)]).reshape(1, LANES)
        lw = dict(
            nh=nh,
            gate_bias=gate_bias, alog_row=alog_row,
            fox_norm_w=fox_norm_w[l].reshape(1, HD),
            gdn_norm_w=gdn_norm_w[l].reshape(1, HD),
            conv_w8=jnp.pad(gdn_conv_w[l], ((0, 8 - CONV_W), (0, 0))),
            w_out=cast_bf16(w_out, l),
            ln_mix_g=ln_mix_g[l], ln_mix_b=ln_mix_b[l],
            ln_ffn_g=ln_ffn_g[l], ln_ffn_b=ln_ffn_b[l],
        )
        xs, xsb, kv_s, st_s = _layer(xs, xsb, bs, ts, l, wts, lw, (k_cache, v_cache, cache_fox_logf[l]),
                                     state_gdn_conv[l], state_gdn[l], kv_s, alpha)
        xp, xpb, kv_p, st_p = _layer(xp, xpb, bp, tp, l, wts, lw, None, zero_conv, zero_state, kv_p, alpha)
        ps.append(st_p)
        ss.append(st_s)

    def stacked(states, i):
        return jnp.stack([st[i] for st in states], axis=0)

    def heads(stack, bsz, t_len):
        return stack.reshape(depth, bsz, t_len, nh, HD)

    return (xp.reshape(bp, tp, d), xs.reshape(bs, ts, d),
            heads(kv_p[0], bp, tp), heads(kv_p[1], bp, tp), stacked(ps, 0), stacked(ps, 1), stacked(ps, 2),
            heads(kv_s[0], bs, ts), heads(kv_s[1], bs, ts), stacked(ss, 0), stacked(ss, 1), stacked(ss, 2))
```
